```python
import jax, jax.numpy as jnp
from jax import lax
import numpy as np

D_MODEL = 2048
BATCH = 8
SEQ = 2048
DEPTH = 4
DEC_BATCH = 16
DEC_SEQ = 2048
PAST_LEN = 128

GRID_W = 64
N_MIXERS = 4
MEM_TOKENS = 256
NORM_EPS = 1e-6
Q_BLOCK = 128

RET_HEADS = 8
RET_DK = D_MODEL // RET_HEADS
RET_DV = 2 * RET_DK
RET_CHUNK = 128
RET_ROPE_BASE = 10000.0
HG_HEADS = 16
HG_DK = 128
HG_DV = D_MODEL // HG_HEADS
HG_CHUNK = 32
MLA_HEADS = 16
MLA_Q_RANK = 512
MLA_KV_RANK = 512
MLA_NOPE = 128
MLA_ROPE = 64
MLA_V = 128
MLA_ROPE_BASE = 10000.0
GQA_HEADS = 16
GQA_KV_HEADS = 4
GQA_HD = 128
GQA_ROPE_BASE = 10000.0
MEM_HEADS = 4
MEM_HD = 128
D_FF = 4 * D_MODEL

N_RET = (DEPTH + N_MIXERS - 1) // N_MIXERS
N_HG = (DEPTH + N_MIXERS - 2) // N_MIXERS
N_MLA = (DEPTH + N_MIXERS - 3) // N_MIXERS
N_GQA = (DEPTH + N_MIXERS - 4) // N_MIXERS

kernel_name = 'hybrid_bidir_encoder_ret_hgrn2_mla_gqa'


def rmsnorm(x, g):
    xf = x.astype(jnp.float32)
    y = xf * lax.rsqrt(jnp.mean(xf * xf, axis=-1, keepdims=True) + NORM_EPS)
    return (y * g.astype(jnp.float32)).astype(x.dtype)


def rope(x, pos, base):
    half = x.shape[-1] // 2
    freqs = base ** (-jnp.arange(half, dtype=jnp.float32) / half)
    ang = pos[:, None] * freqs[None, :]
    cos = jnp.cos(ang)[None, :, None, :].astype(x.dtype)
    sin = jnp.sin(ang)[None, :, None, :].astype(x.dtype)
    x1, x2 = x[..., :half], x[..., half:]
    return jnp.concatenate([x1 * cos - x2 * sin, x1 * sin + x2 * cos], axis=-1)


def axial_rope(x, row_pos, col_pos, base):
    d2 = x.shape[-1] // 2
    return jnp.concatenate([rope(x[..., :d2], row_pos, base), rope(x[..., d2:], col_pos, base)], axis=-1)


def block_attention(q, k, v, scale):
    B, L, KH, G, D = q.shape
    nb = L // Q_BLOCK
    qb = q.reshape(B, nb, Q_BLOCK, KH, G, D).transpose(1, 0, 2, 3, 4, 5)

    def one(qi):
        s = jnp.einsum('bqhgd,bshd->bhgqs', qi, k, preferred_element_type=jnp.float32) * scale
        p = jax.nn.softmax(s, axis=-1).astype(v.dtype)
        return jnp.einsum('bhgqs,bshe->bqhge', p, v)

    o = lax.map(one, qb)
    return o.transpose(1, 0, 2, 3, 4, 5).reshape(B, L, KH, G, v.shape[-1])


def to_chunks(a, chunk):
    B, L, H, d = a.shape
    return a.reshape(B, L // chunk, chunk, H, d).transpose(1, 0, 2, 3, 4)


def from_chunks(a):
    N, B, C, H, d = a.shape
    return a.transpose(1, 0, 2, 3, 4).reshape(B, N * C, H, d)


def retention_scan(q, k, v, log_gamma):
    B, L, H, dk = q.shape
    dv = v.shape[-1]
    C = RET_CHUNK
    idx = jnp.arange(C, dtype=jnp.float32)
    diff = idx[:, None] - idx[None, :]
    decay_intra = jnp.where(diff[None] >= 0, jnp.exp(jnp.maximum(diff, 0.0)[None] * log_gamma[:, None, None]), 0.0)
    q_decay = jnp.exp((idx + 1.0)[:, None] * log_gamma[None, :])
    k_decay = jnp.exp((C - 1.0 - idx)[:, None] * log_gamma[None, :])
    chunk_decay = jnp.exp(C * log_gamma)

    def step(S, xs):
        qc, kc, vc = xs
        s = jnp.einsum('bihd,bjhd->bhij', qc, kc) * decay_intra[None]
        o = jnp.einsum('bhij,bjhe->bihe', s, vc) + jnp.einsum('bihd,bhde->bihe', qc * q_decay[None, :, :, None], S)
        S = S * chunk_decay[None, :, None, None] + jnp.einsum('bjhd,bjhe->bhde', kc * k_decay[None, :, :, None], vc)
        return S, o

    S0 = jnp.zeros((B, H, dk, dv), q.dtype)
    _, o = lax.scan(step, S0, (to_chunks(q, C), to_chunks(k, C), to_chunks(v, C)))
    return from_chunks(o)


def gated_scan(q, k, v, log_f):
    B, L, H, dk = q.shape
    dv = v.shape[-1]
    C = HG_CHUNK
    mask = jnp.tril(jnp.ones((C, C), dtype=bool))

    def step(S, xs):
        qc, kc, vc, lf = xs
        b = jnp.cumsum(lf, axis=1)
        q_t = qc * jnp.exp(b)
        k_t = kc * jnp.exp(-b)
        s = jnp.where(mask[None, None], jnp.einsum('bihd,bjhd->bhij', q_t, k_t), 0.0)
        o = jnp.einsum('bhij,bjhe->bihe', s, vc) + jnp.einsum('bihd,bhde->bihe', q_t, S)
        b_last = b[:, -1:]
        S = S * jnp.exp(b_last[:, 0])[..., None] + jnp.einsum('bjhd,bjhe->bhde', kc * jnp.exp(b_last - b), vc)
        return S, o

    S0 = jnp.zeros((B, H, dk, dv), q.dtype)
    _, o = lax.scan(step, S0, (to_chunks(q, C), to_chunks(k, C), to_chunks(v, C), to_chunks(log_f, C)))
    return from_chunks(o)


def retention_mixer(h, w_in, decay_logit, out_norm, w_out):
    B, L, _ = h.shape
    qk = RET_HEADS * RET_DK
    vw = RET_HEADS * RET_DV
    q, k, v, g = jnp.split(h @ w_in, [qk, 2 * qk, 2 * qk + vw], axis=-1)
    pos = jnp.arange(L, dtype=jnp.float32)
    q = rope(q.reshape(B, L, RET_HEADS, RET_DK), pos, RET_ROPE_BASE).astype(jnp.float32) * (RET_DK ** -0.5)
    k = rope(k.reshape(B, L, RET_HEADS, RET_DK), pos, RET_ROPE_BASE).astype(jnp.float32)
    v = v.reshape(B, L, RET_HEADS, RET_DV).astype(jnp.float32)
    lg = jax.nn.log_sigmoid(decay_logit.astype(jnp.float32))
    o_f = retention_scan(q, k, v, lg[0])
    o_b = jnp.flip(retention_scan(jnp.flip(q, 1), jnp.flip(k, 1), jnp.flip(v, 1), lg[1]), 1)
    o = rmsnorm(o_f + o_b, out_norm).astype(h.dtype) * jax.nn.silu(g).reshape(B, L, RET_HEADS, RET_DV)
    return o.reshape(B, L, vw) @ w_out


def hgrn2_mixer(h, w_in, lb, out_norm, w_out):
    B, L, _ = h.shape
    kw = HG_HEADS * HG_DK
    vw = HG_HEADS * HG_DV
    q, f_fw, f_bw, i, g = jnp.split(h @ w_in, [kw, 2 * kw, 3 * kw, 3 * kw + vw], axis=-1)
    lbf = lb.astype(jnp.float32)

    def gates(fz):
        f = lbf + (1.0 - lbf) * jax.nn.sigmoid(fz.astype(jnp.float32))
        return jnp.log(f).reshape(B, L, HG_HEADS, HG_DK), (1.0 - f).reshape(B, L, HG_HEADS, HG_DK)

    q = q.astype(jnp.float32).reshape(B, L, HG_HEADS, HG_DK)
    iv = i.astype(jnp.float32).reshape(B, L, HG_HEADS, HG_DV)
    lf_f, k_f = gates(f_fw)
    lf_b, k_b = gates(f_bw)
    o_f = gated_scan(q, k_f, iv, lf_f)
    o_b = jnp.flip(gated_scan(jnp.flip(q, 1), jnp.flip(k_b, 1), jnp.flip(iv, 1), jnp.flip(lf_b, 1)), 1)
    o = rmsnorm(o_f + o_b, out_norm).astype(h.dtype) * jax.nn.silu(g).reshape(B, L, HG_HEADS, HG_DV)
    return o.reshape(B, L, vw) @ w_out


def mla_mixer(h, w_in, q_norm, kv_norm, w_qb, w_kvb, qk_norm, w_out):
    B, L, _ = h.shape
    cq, ckv, k_rope = jnp.split(h @ w_in, [MLA_Q_RANK, MLA_Q_RANK + MLA_KV_RANK], axis=-1)
    q = (rmsnorm(cq, q_norm) @ w_qb).reshape(B, L, MLA_HEADS, MLA_NOPE + MLA_ROPE)
    kv = (rmsnorm(ckv, kv_norm) @ w_kvb).reshape(B, L, MLA_HEADS, MLA_NOPE + MLA_V)
    k_nope, v = kv[..., :MLA_NOPE], kv[..., MLA_NOPE:]
    k = jnp.concatenate([k_nope, jnp.broadcast_to(k_rope[:, :, None, :], (B, L, MLA_HEADS, MLA_ROPE))], axis=-1)
    q = rmsnorm(q, qk_norm[0])
    k = rmsnorm(k, qk_norm[1])
    pos = jnp.arange(L, dtype=jnp.float32)
    q = jnp.concatenate([q[..., :MLA_NOPE], rope(q[..., MLA_NOPE:], pos, MLA_ROPE_BASE)], axis=-1)
    k = jnp.concatenate([k[..., :MLA_NOPE], rope(k[..., MLA_NOPE:], pos, MLA_ROPE_BASE)], axis=-1)
    o = block_attention(q[:, :, :, None, :], k, v, (MLA_NOPE + MLA_ROPE) ** -0.5)
    return o.reshape(B, L, MLA_HEADS * MLA_V) @ w_out


def gqa_mixer(h, w_in, qk_norm, w_out):
    B, L, _ = h.shape
    qw = GQA_HEADS * GQA_HD
    kvw = GQA_KV_HEADS * GQA_HD
    q, k, v = jnp.split(h @ w_in, [qw, qw + kvw], axis=-1)
    q = rmsnorm(q.reshape(B, L, GQA_HEADS, GQA_HD), qk_norm[0])
    k = rmsnorm(k.reshape(B, L, GQA_KV_HEADS, GQA_HD), qk_norm[1])
    rows = L // GRID_W
    row_pos = jnp.repeat(jnp.arange(rows), GRID_W).astype(jnp.float32)
    col_pos = jnp.tile(jnp.arange(GRID_W), rows).astype(jnp.float32)
    q = axial_rope(q, row_pos, col_pos, GQA_ROPE_BASE)
    k = axial_rope(k, row_pos, col_pos, GQA_ROPE_BASE)
    q = q.reshape(B, L, GQA_KV_HEADS, GQA_HEADS // GQA_KV_HEADS, GQA_HD)
    o = block_attention(q, k, v.reshape(B, L, GQA_KV_HEADS, GQA_HD), GQA_HD ** -0.5)
    return o.reshape(B, L, qw) @ w_out


def memory_xattn(h, m, w_q, w_kv, qk_norm, w_out):
    B, L, _ = h.shape
    M = m.shape[1]
    q = rmsnorm((h @ w_q).reshape(B, L, MEM_HEADS, MEM_HD), qk_norm[0])
    kv = (m @ w_kv).reshape(B, M, 2, MEM_HEADS, MEM_HD)
    k = rmsnorm(kv[:, :, 0], qk_norm[1])
    v = kv[:, :, 1]
    s = jnp.einsum('blhd,bmhd->bhlm', q, k, preferred_element_type=jnp.float32) * (MEM_HD ** -0.5)
    p = jax.nn.softmax(s, axis=-1).astype(v.dtype)
    o = jnp.einsum('bhlm,bmhd->blhd', p, v)
    return o.reshape(B, L, MEM_HEADS * MEM_HD) @ w_out


def sq_relu_mlp(h, w1, w2):
    a = jax.nn.relu(h @ w1)
    return (a * a) @ w2


def trunk(x, mem, p):
    s = jax.nn.softmax(p['hg_lb'].astype(jnp.float32), axis=0)
    lb_all = jnp.cumsum(s, axis=0) - s[0]
    for i in range(DEPTH):
        kind, j = i % N_MIXERS, i // N_MIXERS
        h = rmsnorm(x, p['norm_mix'][i])
        if kind == 0:
            x = x + retention_mixer(h, p['ret_w_in'][j], p['ret_decay'][j], p['ret_out_norm'][j], p['ret_w_out'][j])
        elif kind == 1:
            x = x + hgrn2_mixer(h, p['hg_w_in'][j], lb_all[i], p['hg_out_norm'][j], p['hg_w_out'][j])
        elif kind == 2:
            x = x + mla_mixer(h, p['mla_w_in'][j], p['mla_q_norm'][j], p['mla_kv_norm'][j], p['mla_w_qb'][j],
                              p['mla_w_kvb'][j], p['mla_qk_norm'][j], p['mla_w_out'][j])
        else:
            x = x + gqa_mixer(h, p['gqa_w_in'][j], p['gqa_qk_norm'][j], p['gqa_w_out'][j])
        h = rmsnorm(x, p['norm_mem'][i])
        m = rmsnorm(mem, p['norm_memtok'][i])
        x = x + memory_xattn(h, m, p['mem_w_q'][i], p['mem_w_kv'][i], p['mem_qk_norm'][i], p['mem_w_out'][i])
        h = rmsnorm(x, p['norm_mlp'][i])
        x = x + sq_relu_mlp(h, p['mlp_w1'][i], p['mlp_w2'][i])
    return x


def setup_inputs(seed: int = 0) -> dict:
    key = jax.random.key(seed)
    ks = iter(jax.random.split(key, 40))

    def nrm(shape, scale):
        return scale * jax.random.normal(next(ks), shape, jnp.float32)

    def dense(shape):
        return nrm(shape, shape[-2] ** -0.5)

    def gain(shape):
        return 1.0 + nrm(shape, 0.02)

    hidx = jnp.arange(RET_HEADS, dtype=jnp.float32)
    ret_logit0 = jnp.log(2.0 ** (5.0 + hidx) - 1.0)
    ret_in_w = 2 * RET_HEADS * RET_DK + 2 * RET_HEADS * RET_DV
    hg_in_w = 3 * HG_HEADS * HG_DK + 2 * HG_HEADS * HG_DV
    return {
        'x_prompt': nrm((BATCH, SEQ, D_MODEL), 1.0),
        'x_sample': nrm((DEC_BATCH, DEC_SEQ, D_MODEL), 1.0),
        'mem_prompt': nrm((BATCH, MEM_TOKENS, D_MODEL), 1.0),
        'mem_sample': nrm((DEC_BATCH, MEM_TOKENS, D_MODEL), 1.0),
        'norm_mix': gain((DEPTH, D_MODEL)),
        'norm_mem': gain((DEPTH, D_MODEL)),
        'norm_memtok': gain((DEPTH, D_MODEL)),
        'norm_mlp': gain((DEPTH, D_MODEL)),
        'ret_w_in': dense((N_RET, D_MODEL, ret_in_w)),
        'ret_decay': ret_logit0[None, None, :] + nrm((N_RET, 2, RET_HEADS), 0.05),
        'ret_out_norm': gain((N_RET, RET_DV)),
        'ret_w_out': dense((N_RET, RET_HEADS * RET_DV, D_MODEL)),
        'hg_w_in': dense((N_HG, D_MODEL, hg_in_w)),
        'hg_lb': nrm((DEPTH, HG_HEADS * HG_DK), 0.1),
        'hg_out_norm': gain((N_HG, HG_DV)),
        'hg_w_out': dense((N_HG, HG_HEADS * HG_DV, D_MODEL)),
        'mla_w_in': dense((N_MLA, D_MODEL, MLA_Q_RANK + MLA_KV_RANK + MLA_ROPE)),
        'mla_q_norm': gain((N_MLA, MLA_Q_RANK)),
        'mla_kv_norm': gain((N_MLA, MLA_KV_RANK)),
        'mla_w_qb': dense((N_MLA, MLA_Q_RANK, MLA_HEADS * (MLA_NOPE + MLA_ROPE))),
        'mla_w_kvb': dense((N_MLA, MLA_KV_RANK, MLA_HEADS * (MLA_NOPE + MLA_V))),
        'mla_qk_norm': gain((N_MLA, 2, MLA_NOPE + MLA_ROPE)),
        'mla_w_out': dense((N_MLA, MLA_HEADS * MLA_V, D_MODEL)),
        'gqa_w_in': dense((N_GQA, D_MODEL, (GQA_HEADS + 2 * GQA_KV_HEADS) * GQA_HD)),
        'gqa_qk_norm': gain((N_GQA, 2, GQA_HD)),
        'gqa_w_out': dense((N_GQA, GQA_HEADS * GQA_HD, D_MODEL)),
        'mem_w_q': dense((DEPTH, D_MODEL, MEM_HEADS * MEM_HD)),
        'mem_w_kv': dense((DEPTH, D_MODEL, 2 * MEM_HEADS * MEM_HD)),
        'mem_qk_norm': gain((DEPTH, 2, MEM_HD)),
        'mem_w_out': dense((DEPTH, MEM_HEADS * MEM_HD, D_MODEL)),
        'mlp_w1': dense((DEPTH, D_MODEL, D_FF)),
        'mlp_w2': dense((DEPTH, D_FF, D_MODEL)),
    }


def reference(x_prompt, x_sample, mem_prompt, mem_sample, norm_mix, norm_mem, norm_memtok, norm_mlp,
              ret_w_in, ret_decay, ret_out_norm, ret_w_out, hg_w_in, hg_lb, hg_out_norm, hg_w_out,
              mla_w_in, mla_q_norm, mla_kv_norm, mla_w_qb, mla_w_kvb, mla_qk_norm, mla_w_out,
              gqa_w_in, gqa_qk_norm, gqa_w_out, mem_w_q, mem_w_kv, mem_qk_norm, mem_w_out, mlp_w1, mlp_w2):
    params = dict(norm_mix=norm_mix, norm_mem=norm_mem, norm_memtok=norm_memtok, norm_mlp=norm_mlp,
                  ret_w_in=ret_w_in, ret_decay=ret_decay, ret_out_norm=ret_out_norm, ret_w_out=ret_w_out,
                  hg_w_in=hg_w_in, hg_lb=hg_lb, hg_out_norm=hg_out_norm, hg_w_out=hg_w_out,
                  mla_w_in=mla_w_in, mla_q_norm=mla_q_norm, mla_kv_norm=mla_kv_norm, mla_w_qb=mla_w_qb,
                  mla_w_kvb=mla_w_kvb, mla_qk_norm=mla_qk_norm, mla_w_out=mla_w_out,
                  gqa_w_in=gqa_w_in, gqa_qk_norm=gqa_qk_norm, gqa_w_out=gqa_w_out,
                  mem_w_q=mem_w_q, mem_w_kv=mem_w_kv, mem_qk_norm=mem_qk_norm, mem_w_out=mem_w_out,
                  mlp_w1=mlp_w1, mlp_w2=mlp_w2)
    y_prompt = trunk(x_prompt, mem_prompt, params)
    y_sample = trunk(x_sample, mem_sample, params)
    return (y_prompt, y_sample)
```

```python
import functools

import jax
import jax.numpy as jnp
from jax import lax
from jax.experimental import pallas as pl
from jax.experimental.pallas import tpu as pltpu

F32 = jnp.float32
BF16 = jnp.bfloat16

D_MODEL = 2048
GRID_W = 64
NORM_EPS = 1e-6
RET_HEADS, RET_DK, RET_DV = 8, 256, 512
RET_ROPE_BASE = 10000.0
HG_HEADS, HG_DK, HG_DV, HG_CHUNK = 16, 128, 128, 32
MLA_HEADS, MLA_Q_RANK, MLA_KV_RANK, MLA_NOPE, MLA_ROPE, MLA_V = 16, 512, 512, 128, 64, 128
MLA_ROPE_BASE = 10000.0
GQA_HEADS, GQA_KV_HEADS, GQA_HD = 16, 4, 128
GQA_ROPE_BASE = 10000.0
MEM_HEADS, MEM_HD = 4, 128

LANES = 128
VMEM_LIMIT_BYTES = 56 * 2 ** 20

HG_BLOCK = 128
HG_HEADS_PER_STEP = 4


def _params(sem):
    return pltpu.CompilerParams(dimension_semantics=sem, vmem_limit_bytes=VMEM_LIMIT_BYTES)


def _tile(n, pref):
    return pref if n % pref == 0 else n


def _rms_scale(x, width=None):
    width = x.shape[-1] if width is None else width
    ms = jnp.sum(x * x, axis=-1, keepdims=True) * (1.0 / width)
    return x * lax.rsqrt(ms + NORM_EPS)


def _sigmoid(x):
    return 1.0 / (1.0 + jnp.exp(-x))


def _dot(a, b):
    return jnp.dot(a, b, preferred_element_type=F32)


def _dot_nt(a, b):
    return lax.dot_general(a, b, (((1,), (1,)), ((), ())), preferred_element_type=F32)


def _rope_pairs32(x, c, s_lo, s_hi):
    return x * c + pltpu.roll(x, 32, 1) * s_hi + pltpu.roll(x, 96, 1) * s_lo


def _norm_matmul_kernel(x_ref, g_ref, w_ref, o_ref, xn_ref):
    @pl.when(pl.program_id(1) == 0)
    def _():
        x = x_ref[...].astype(F32)
        xn_ref[...] = (_rms_scale(x) * g_ref[...]).astype(BF16)

    o_ref[...] = _dot(xn_ref[...], w_ref[...]).astype(o_ref.dtype)


def _norm_matmul(x, g, w, out_dtype, tm=1024, tn=512):
    T, K = x.shape
    N = w.shape[1]
    tm, tn = _tile(T, tm), _tile(N, tn)
    return pl.pallas_call(
        _norm_matmul_kernel,
        grid=(T // tm, N // tn),
        in_specs=[pl.BlockSpec((tm, K), lambda i, j: (i, 0)),
                  pl.BlockSpec((1, K), lambda i, j: (0, 0)),
                  pl.BlockSpec((K, tn), lambda i, j: (0, j))],
        out_specs=pl.BlockSpec((tm, tn), lambda i, j: (i, j)),
        out_shape=jax.ShapeDtypeStruct((T, N), out_dtype),
        scratch_shapes=[pltpu.VMEM((tm, K), BF16)],
        compiler_params=_params(("parallel", "arbitrary")),
        name="norm_matmul",
    )(x, g.reshape(1, K).astype(F32), w)


def _matmul_residual_kernel(a_ref, w_ref, x_ref, o_ref):
    o_ref[...] = x_ref[...] + _dot(a_ref[...], w_ref[...])


def _matmul_residual(a, w, x, tm=1024, tn=512):
    T, K = a.shape
    N = w.shape[1]
    tm, tn = _tile(T, tm), _tile(N, tn)
    return pl.pallas_call(
        _matmul_residual_kernel,
        grid=(T // tm, N // tn),
        in_specs=[pl.BlockSpec((tm, K), lambda i, j: (i, 0)),
                  pl.BlockSpec((K, tn), lambda i, j: (0, j)),
                  pl.BlockSpec((tm, tn), lambda i, j: (i, j))],
        out_specs=pl.BlockSpec((tm, tn), lambda i, j: (i, j)),
        out_shape=jax.ShapeDtypeStruct((T, N), F32),
        input_output_aliases={2: 0},
        compiler_params=_params(("parallel", "arbitrary")),
        name="matmul_residual",
    )(a, w, x)


def _mlp_kernel(x_ref, g_ref, w1_ref, w2_ref, o_ref, xn_ref):
    @pl.when(pl.program_id(1) == 0)
    def _():
        x = x_ref[...]
        xn_ref[...] = (_rms_scale(x) * g_ref[...]).astype(BF16)
        o_ref[...] = x

    a = jnp.maximum(_dot(xn_ref[...], w1_ref[...]), 0.0)
    o_ref[...] += _dot((a * a).astype(BF16), w2_ref[...])


def _mlp(x, g, w1, w2, tm=512, tf=1024):
    T, D = x.shape
    Fd = w1.shape[1]
    tm, tf = _tile(T, tm), _tile(Fd, tf)
    return pl.pallas_call(
        _mlp_kernel,
        grid=(T // tm, Fd // tf),
        in_specs=[pl.BlockSpec((tm, D), lambda i, f: (i, 0)),
                  pl.BlockSpec((1, D), lambda i, f: (0, 0)),
                  pl.BlockSpec((D, tf), lambda i, f: (0, f)),
                  pl.BlockSpec((tf, D), lambda i, f: (f, 0))],
        out_specs=pl.BlockSpec((tm, D), lambda i, f: (i, 0)),
        out_shape=jax.ShapeDtypeStruct((T, D), F32),
        scratch_shapes=[pltpu.VMEM((tm, D), BF16)],
        compiler_params=_params(("parallel", "arbitrary")),
        name="mlp",
    )(x, g.reshape(1, D).astype(F32), w1, w2)


def _xattn_kernel(x_ref, g_ref, wq_ref, kv_ref, gq_ref, gk_ref, wo_ref, o_ref):
    x = x_ref[0]
    xn = (_rms_scale(x) * g_ref[...]).astype(BF16)
    q = _dot(xn, wq_ref[...])
    kv = kv_ref[0].astype(F32)
    width = MEM_HEADS * MEM_HD
    outs = []
    for h in range(MEM_HEADS):
        cs = slice(h * MEM_HD, (h + 1) * MEM_HD)
        qh = (_rms_scale(q[:, cs]) * gq_ref[...]).astype(BF16)
        kh = (_rms_scale(kv[:, cs]) * gk_ref[...]).astype(BF16)
        vh = kv_ref[0, :, width + h * MEM_HD: width + (h + 1) * MEM_HD]
        s = _dot_nt(qh, kh)
        p = jnp.exp(s - jnp.max(s, axis=-1, keepdims=True))
        l = jnp.sum(p, axis=-1, keepdims=True)
        outs.append((_dot(p.astype(BF16), vh) / l).astype(BF16))
    o = jnp.concatenate(outs, axis=-1)
    o_ref[0] = x + _dot(o, wo_ref[...])


def _xattn(x, kv, g, wq, gq, gk, wo, tm=512):
    B, L, D = x.shape
    M = kv.shape[1]
    width = MEM_HEADS * MEM_HD
    tm = _tile(L, tm)
    return pl.pallas_call(
        _xattn_kernel,
        grid=(B, L // tm),
        in_specs=[pl.BlockSpec((1, tm, D), lambda b, i: (b, i, 0)),
                  pl.BlockSpec((1, D), lambda b, i: (0, 0)),
                  pl.BlockSpec((D, width), lambda b, i: (0, 0)),
                  pl.BlockSpec((1, M, 2 * width), lambda b, i: (b, 0, 0)),
                  pl.BlockSpec((1, MEM_HD), lambda b, i: (0, 0)),
                  pl.BlockSpec((1, MEM_HD), lambda b, i: (0, 0)),
                  pl.BlockSpec((width, D), lambda b, i: (0, 0))],
        out_specs=pl.BlockSpec((1, tm, D), lambda b, i: (b, i, 0)),
        out_shape=jax.ShapeDtypeStruct((B, L, D), F32),
        input_output_aliases={0: 0},
        compiler_params=_params(("parallel", "arbitrary")),
        name="mem_xattn",
    )(x, g.reshape(1, D).astype(F32), wq, kv,
      (gq.astype(F32) * (MEM_HD ** -0.5)).reshape(1, MEM_HD), gk.reshape(1, MEM_HD).astype(F32), wo)


def _attn_kernel(q_ref, k_ref, v_ref, o_ref, *, groups, dq, dv):
    k = k_ref[0]
    v = v_ref[0]
    for g in range(groups):
        q = q_ref[0, :, g * dq:(g + 1) * dq]
        s = _dot_nt(q, k)
        p = jnp.exp(s - jnp.max(s, axis=-1, keepdims=True))
        l = jnp.sum(p, axis=-1, keepdims=True)
        o_ref[0, :, g * dv:(g + 1) * dv] = (_dot(p.astype(BF16), v) / l).astype(o_ref.dtype)


def _attention(q, k, v, *, kv_heads, groups, dq, dv, k_block0, v_block0, tq=512):
    B, L = q.shape[0], q.shape[1]
    tq = _tile(L, tq)
    return pl.pallas_call(
        functools.partial(_attn_kernel, groups=groups, dq=dq, dv=dv),
        grid=(B, kv_heads, L // tq),
        in_specs=[pl.BlockSpec((1, tq, groups * dq), lambda b, h, i: (b, i, h)),
                  pl.BlockSpec((1, L, dq), lambda b, h, i: (b, 0, k_block0 + h)),
                  pl.BlockSpec((1, L, dv), lambda b, h, i: (b, 0, v_block0 + h))],
        out_specs=pl.BlockSpec((1, tq, groups * dv), lambda b, h, i: (b, i, h)),
        out_shape=jax.ShapeDtypeStruct((B, L, kv_heads * groups * dv), BF16),
        compiler_params=_params(("parallel", "parallel", "arbitrary")),
        name="softmax_attention",
    )(q, k, v)


def _retention_kernel(q_ref, k_ref, v_ref, g_ref, cos_ref, sin_ref, dec_ref, on_ref, o_ref, *, tq):
    L = q_ref.shape[1]
    half = RET_DK // 2
    cos = cos_ref[...]
    sin = sin_ref[...]

    def rope(ref, scale):
        x1 = ref[0, :, :half].astype(F32)
        x2 = ref[0, :, half:].astype(F32)
        y = jnp.concatenate([x1 * cos - x2 * sin, x1 * sin + x2 * cos], axis=-1)
        return (y * scale).astype(BF16)

    q = rope(q_ref, RET_DK ** -0.5)
    k = rope(k_ref, 1.0)
    v = v_ref[0]
    dec = dec_ref[0]
    lg = jnp.minimum(dec, 0.0) - jnp.log1p(jnp.exp(-jnp.abs(dec)))
    lg_f = lg[0:1, 0:1]
    lg_b = lg[1:2, 0:1]
    for i in range(L // tq):
        rows = slice(i * tq, (i + 1) * tq)
        s = _dot_nt(q[rows], k)
        t = lax.broadcasted_iota(jnp.int32, (tq, L), 0) + i * tq
        diff = (t - lax.broadcasted_iota(jnp.int32, (tq, L), 1)).astype(F32)
        decay = jnp.exp(jnp.where(diff >= 0, diff * lg_f, -diff * lg_b))
        decay = jnp.where(diff == 0, 2.0, decay)
        o = _dot((s * decay).astype(BF16), v)
        y = _rms_scale(o) * on_ref[...]
        gate = g_ref[0, rows, :].astype(F32)
        o_ref[0, rows, :] = (y * (gate * _sigmoid(gate))).astype(o_ref.dtype)


def _retention(p, cos, sin, dec, out_norm, tq=512):
    B, L, _ = p.shape
    H = RET_HEADS
    tq = _tile(L, tq)
    v_block0 = 2 * H * RET_DK // RET_DV
    return pl.pallas_call(
        functools.partial(_retention_kernel, tq=tq),
        grid=(H, B),
        in_specs=[pl.BlockSpec((1, L, RET_DK), lambda h, b: (b, 0, h)),
                  pl.BlockSpec((1, L, RET_DK), lambda h, b: (b, 0, H + h)),
                  pl.BlockSpec((1, L, RET_DV), lambda h, b: (b, 0, v_block0 + h)),
                  pl.BlockSpec((1, L, RET_DV), lambda h, b: (b, 0, v_block0 + H + h)),
                  pl.BlockSpec((L, RET_DK // 2), lambda h, b: (0, 0)),
                  pl.BlockSpec((L, RET_DK // 2), lambda h, b: (0, 0)),
                  pl.BlockSpec((1, 2, LANES), lambda h, b: (h, 0, 0)),
                  pl.BlockSpec((1, RET_DV), lambda h, b: (0, 0))],
        out_specs=pl.BlockSpec((1, L, RET_DV), lambda h, b: (b, 0, h)),
        out_shape=jax.ShapeDtypeStruct((B, L, H * RET_DV), BF16),
        compiler_params=_params(("parallel", "arbitrary")),
        name="retention",
    )(p, p, p, p, cos, sin, dec, out_norm.reshape(1, RET_DV).astype(F32))


def _hgrn_kernel(q_ref, ff_ref, fb_ref, i_ref, g_ref, lb_ref, on_ref, m_ref, o_ref,
                 of_ref, ob_ref, st_ref, *, heads):
    L = q_ref.shape[1]
    nblk = L // HG_BLOCK
    n_chunks = HG_BLOCK // HG_CHUNK
    rows = lax.broadcasted_iota(jnp.int32, (HG_BLOCK, HG_BLOCK), 0)
    cols = lax.broadcasted_iota(jnp.int32, (HG_BLOCK, HG_BLOCK), 1)
    same_chunk = (rows // HG_CHUNK) == (cols // HG_CHUNK)
    masks = (same_chunk & (cols <= rows), same_chunk & (cols >= rows))
    row_chunk = lax.broadcasted_iota(jnp.int32, (HG_BLOCK, HG_DK), 0) // HG_CHUNK
    st_ref[...] = jnp.zeros_like(st_ref)

    def block(r, h, d):
        r0 = pl.multiple_of(r * HG_BLOCK, HG_BLOCK)
        rs = pl.ds(r0, HG_BLOCK)
        cs = slice(h * HG_DK, (h + 1) * HG_DK)
        fz = (ff_ref if d == 0 else fb_ref)[0, rs, cs].astype(F32)
        lb = lb_ref[:, cs]
        f = lb + (1.0 - lb) * _sigmoid(fz)
        lf = jnp.log(f)
        kk = 1.0 - f
        hi = lf.astype(BF16)
        lo = (lf - hi.astype(F32)).astype(BF16)
        cb = _dot(m_ref[d], hi) + _dot(m_ref[d], lo)
        b = cb[:HG_BLOCK]
        bl = cb[HG_BLOCK:]
        q = q_ref[0, rs, cs].astype(F32)
        v = i_ref[0, rs, cs]
        qt = (q * jnp.exp(b)).astype(BF16)
        kt = (kk * jnp.exp(-b)).astype(BF16)
        ks = kk * jnp.exp(bl - b)
        s = jnp.where(masks[d], _dot_nt(qt, kt), 0.0).astype(BF16)
        o = _dot(s, v)
        vt = v.astype(F32).T.astype(BF16)
        st = st_ref[2 * h + d]
        parts = [None] * n_chunks
        order = range(n_chunks) if d == 0 else reversed(range(n_chunks))
        for c in order:
            cr = slice(c * HG_CHUNK, (c + 1) * HG_CHUNK)
            parts[c] = o[cr] + _dot_nt(qt[cr], st.astype(BF16))
            ksm = jnp.where(row_chunk == c, ks, 0.0).astype(BF16)
            st = st * jnp.exp(bl[c * HG_CHUNK:c * HG_CHUNK + 1, :]) + _dot(vt, ksm)
        st_ref[2 * h + d] = st
        (of_ref if d == 0 else ob_ref)[rs, cs] = jnp.concatenate(parts, axis=0)

    def body(t, carry):
        for h in range(heads):
            block(t, h, 0)
            block(nblk - 1 - t, h, 1)
        return carry

    lax.fori_loop(0, nblk, body, 0)
    for h in range(heads):
        cs = slice(h * HG_DV, (h + 1) * HG_DV)
        y = _rms_scale(of_ref[:, cs] + ob_ref[:, cs]) * on_ref[...]
        gate = g_ref[0, :, cs].astype(F32)
        o_ref[0, :, cs] = (y * (gate * _sigmoid(gate))).astype(o_ref.dtype)


def _hgrn(p, lb, out_norm):
    B, L, _ = p.shape
    heads = HG_HEADS_PER_STEP
    nb = HG_HEADS // heads
    w = heads * HG_DK
    c = HG_CHUNK
    idx = jnp.arange(HG_BLOCK)
    same = (idx[:, None] // c) == (idx[None, :] // c)
    tri_f = same & (idx[None, :] <= idx[:, None])
    tri_b = same & (idx[None, :] >= idx[:, None])
    m = jnp.stack([jnp.concatenate([tri_f, same], 0), jnp.concatenate([tri_b, same], 0)]).astype(BF16)
    spec = lambda off: pl.BlockSpec((1, L, w), lambda b, j: (b, 0, off * nb + j))
    return pl.pallas_call(
        functools.partial(_hgrn_kernel, heads=heads),
        grid=(B, nb),
        in_specs=[spec(0), spec(1), spec(2), spec(3), spec(4),
                  pl.BlockSpec((1, w), lambda b, j: (0, j)),
                  pl.BlockSpec((1, HG_DV), lambda b, j: (0, 0)),
                  pl.BlockSpec((2, 2 * HG_BLOCK, HG_BLOCK), lambda b, j: (0, 0, 0))],
        out_specs=pl.BlockSpec((1, L, w), lambda b, j: (b, 0, j)),
        out_shape=jax.ShapeDtypeStruct((B, L, HG_HEADS * HG_DV), BF16),
        scratch_shapes=[pltpu.VMEM((L, w), F32), pltpu.VMEM((L, w), F32),
                        pltpu.VMEM((2 * heads, HG_DV, HG_DK), F32)],
        compiler_params=_params(("parallel", "arbitrary")),
        name="hgrn2",
    )(p, p, p, p, p, lb.reshape(1, -1).astype(F32), out_norm.reshape(1, HG_DV).astype(F32), m)


def _mla_q_kernel(c_ref, gn_ref, w_ref, gh_ref, cos_ref, slo_ref, shi_ref, o_ref, xn_ref):
    @pl.when(pl.program_id(1) == 0)
    def _():
        xn_ref[...] = (_rms_scale(c_ref[...]) * gn_ref[...]).astype(BF16)

    y = _dot(xn_ref[...], w_ref[...])
    y = _rms_scale(y, MLA_NOPE + MLA_ROPE) * gh_ref[...]
    o_ref[:, :MLA_NOPE] = y[:, :MLA_NOPE].astype(o_ref.dtype)
    o_ref[:, MLA_NOPE:] = _rope_pairs32(y[:, MLA_NOPE:], cos_ref[...], slo_ref[...], shi_ref[...]).astype(o_ref.dtype)


def _mla_kv_kernel(c_ref, kr_ref, gn_ref, w_ref, gh_ref, cos_ref, slo_ref, shi_ref, k_ref, v_ref, xn_ref):
    @pl.when(pl.program_id(1) == 0)
    def _():
        xn_ref[...] = (_rms_scale(c_ref[...]) * gn_ref[...]).astype(BF16)

    y = _dot(xn_ref[...], w_ref[...])
    kn = y[:, :MLA_NOPE]
    kr = kr_ref[...]
    ms = (jnp.sum(kn * kn, axis=-1, keepdims=True) + jnp.sum(kr * kr, axis=-1, keepdims=True)) * (
        1.0 / (MLA_NOPE + MLA_ROPE))
    inv = lax.rsqrt(ms + NORM_EPS)
    gh = gh_ref[...]
    k_ref[:, :MLA_NOPE] = (kn * inv * gh[:, :MLA_NOPE]).astype(k_ref.dtype)
    k_ref[:, MLA_NOPE:] = _rope_pairs32(kr * inv * gh[:, MLA_NOPE:], cos_ref[...], slo_ref[...],
                                        shi_ref[...]).astype(k_ref.dtype)
    v_ref[...] = y[:, MLA_NOPE:].astype(v_ref.dtype)


def _mla_qkv(c, L, q_norm, kv_norm, wq, wkv, gq, gk, tabs, tm=1024):
    T = c.shape[0]
    H = MLA_HEADS
    tm = _tile(L, tm)
    lt = L // tm
    hw = 2 * LANES
    tab_spec = pl.BlockSpec((tm, LANES), lambda i, h: (i % lt, 0))
    row = lambda n: pl.BlockSpec((1, n), lambda i, h: (0, 0))
    q = pl.pallas_call(
        _mla_q_kernel,
        grid=(T // tm, H),
        in_specs=[pl.BlockSpec((tm, MLA_Q_RANK), lambda i, h: (i, 0)), row(MLA_Q_RANK),
                  pl.BlockSpec((MLA_Q_RANK, hw), lambda i, h: (0, h)), row(hw),
                  tab_spec, tab_spec, tab_spec],
        out_specs=pl.BlockSpec((tm, hw), lambda i, h: (i, h)),
        out_shape=jax.ShapeDtypeStruct((T, H * hw), BF16),
        scratch_shapes=[pltpu.VMEM((tm, MLA_Q_RANK), BF16)],
        compiler_params=_params(("parallel", "arbitrary")),
        name="mla_q",
    )(c, q_norm.reshape(1, -1).astype(F32), wq, gq, *tabs)
    k, v = pl.pallas_call(
        _mla_kv_kernel,
        grid=(T // tm, H),
        in_specs=[pl.BlockSpec((tm, MLA_KV_RANK), lambda i, h: (i, 1)),
                  pl.BlockSpec((tm, LANES), lambda i, h: (i, (MLA_Q_RANK + MLA_KV_RANK) // LANES)),
                  row(MLA_KV_RANK),
                  pl.BlockSpec((MLA_KV_RANK, hw), lambda i, h: (0, h)), row(hw),
                  tab_spec, tab_spec, tab_spec],
        out_specs=[pl.BlockSpec((tm, hw), lambda i, h: (i, h)),
                   pl.BlockSpec((tm, MLA_V), lambda i, h: (i, h))],
        out_shape=[jax.ShapeDtypeStruct((T, H * hw), BF16), jax.ShapeDtypeStruct((T, H * MLA_V), BF16)],
        scratch_shapes=[pltpu.VMEM((tm, MLA_KV_RANK), BF16)],
        compiler_params=_params(("parallel", "arbitrary")),
        name="mla_kv",
    )(c, c, kv_norm.reshape(1, -1).astype(F32), wkv, gk, *tabs)
    return q, k, v


def _head_norm_rope_kernel(x_ref, g_ref, cos_ref, slo_ref, shi_ref, o_ref):
    y = _rms_scale(x_ref[...].astype(F32)) * g_ref[...]
    o_ref[...] = _rope_pairs32(y, cos_ref[...], slo_ref[...], shi_ref[...]).astype(o_ref.dtype)


def _head_norm_rope(p, L, n_heads, gains, tabs, tm=1024):
    T = p.shape[0]
    tm = _tile(L, tm)
    lt = L // tm
    tab_spec = pl.BlockSpec((tm, LANES), lambda i, h: (i % lt, 0))
    return pl.pallas_call(
        _head_norm_rope_kernel,
        grid=(T // tm, n_heads),
        in_specs=[pl.BlockSpec((tm, LANES), lambda i, h: (i, h)),
                  pl.BlockSpec((1, LANES), lambda i, h: (0, h)),
                  tab_spec, tab_spec, tab_spec],
        out_specs=pl.BlockSpec((tm, LANES), lambda i, h: (i, h)),
        out_shape=jax.ShapeDtypeStruct((T, n_heads * LANES), BF16),
        compiler_params=_params(("parallel", "arbitrary")),
        name="head_norm_rope",
    )(p, gains, *tabs)


def _rope_tables32(pos_a, pos_b, base):
    half = 32
    freqs = base ** (-jnp.arange(half, dtype=F32) / half)

    def cs(pos):
        ang = pos[:, None] * freqs[None, :]
        return jnp.cos(ang), jnp.sin(ang)

    ca, sa = cs(pos_a)
    z = jnp.zeros_like(ca)
    if pos_b is None:
        cb, sb = z, z
    else:
        cb, sb = cs(pos_b)
    cos = jnp.concatenate([ca, ca, cb, cb], axis=-1)
    s_lo = jnp.concatenate([-sa, z, -sb, z], axis=-1)
    s_hi = jnp.concatenate([z, sa, z, sb], axis=-1)
    return cos, s_lo, s_hi


def _trunk(x, mem, w):
    B, L, D = x.shape
    T = B * L
    M = mem.shape[1]
    x = x.reshape(T, D)
    mem = mem.reshape(B * M, D)
    pos = jnp.arange(L, dtype=F32)
    sm = jax.nn.softmax(w['hg_lb'].astype(F32), axis=0)
    lb_all = jnp.cumsum(sm, axis=0) - sm[0]
    depth = w['norm_mix'].shape[0]
    for i in range(depth):
        kind, j = i % 4, i // 4
        if kind == 0:
            p = _norm_matmul(x, w['norm_mix'][i], w['ret_w_in'][j], BF16)
            half = RET_DK // 2
            ang = pos[:, None] * (RET_ROPE_BASE ** (-jnp.arange(half, dtype=F32) / half))[None, :]
            dec = jnp.broadcast_to(w['ret_decay'][j].astype(F32).T[:, :, None], (RET_HEADS, 2, LANES))
            o = _retention(p.reshape(B, L, -1), jnp.cos(ang), jnp.sin(ang), dec, w['ret_out_norm'][j])
            x = _matmul_residual(o.reshape(T, -1), w['ret_w_out'][j], x)
        elif kind == 1:
            p = _norm_matmul(x, w['norm_mix'][i], w['hg_w_in'][j], BF16)
            o = _hgrn(p.reshape(B, L, -1), lb_all[i], w['hg_out_norm'][j])
            x = _matmul_residual(o.reshape(T, -1), w['hg_w_out'][j], x)
        elif kind == 2:
            c = _norm_matmul(x, w['norm_mix'][i], w['mla_w_in'][j], F32, tn=384)
            tabs = _rope_tables32(pos, None, MLA_ROPE_BASE)
            q, k, v = _mla_qkv(c, L, w['mla_q_norm'][j], w['mla_kv_norm'][j], w['mla_w_qb'][j], w['mla_w_kvb'][j],
                               w['mla_gq'][j], w['mla_gk'][j], tabs)
            o = _attention(q.reshape(B, L, -1), k.reshape(B, L, -1), v.reshape(B, L, -1),
                           kv_heads=MLA_HEADS, groups=1, dq=2 * LANES, dv=MLA_V, k_block0=0, v_block0=0)
            x = _matmul_residual(o.reshape(T, -1), w['mla_w_out'][j], x)
        else:
            p = _norm_matmul(x, w['norm_mix'][i], w['gqa_w_in'][j], BF16)
            t = jnp.arange(L)
            tabs = _rope_tables32((t // GRID_W).astype(F32), (t % GRID_W).astype(F32), GQA_ROPE_BASE)
            nqk = GQA_HEADS + GQA_KV_HEADS
            qk = _head_norm_rope(p, L, nqk, w['gqa_gains'][j], tabs)
            o = _attention(qk.reshape(B, L, -1), qk.reshape(B, L, -1), p.reshape(B, L, -1),
                           kv_heads=GQA_KV_HEADS, groups=GQA_HEADS // GQA_KV_HEADS, dq=GQA_HD, dv=GQA_HD,
                           k_block0=GQA_HEADS, v_block0=nqk)
            x = _matmul_residual(o.reshape(T, -1), w['gqa_w_out'][j], x)
        kv = _norm_matmul(mem, w['norm_memtok'][i], w['mem_w_kv'][i], BF16)
        x = _xattn(x.reshape(B, L, D), kv.reshape(B, M, -1), w['norm_mem'][i], w['mem_w_q'][i],
                   w['mem_qk_norm'][i, 0], w['mem_qk_norm'][i, 1], w['mem_w_out'][i]).reshape(T, D)
        x = _mlp(x, w['norm_mlp'][i], w['mlp_w1'][i], w['mlp_w2'][i])
    return x.reshape(B, L, D)


def _prepare_weights(w):
    out = dict(w)
    for name in ('ret_w_in', 'ret_w_out', 'hg_w_in', 'hg_w_out', 'mla_w_out', 'gqa_w_in', 'gqa_w_out',
                 'mem_w_q', 'mem_w_kv', 'mem_w_out', 'mlp_w1', 'mlp_w2', 'mla_w_kvb'):
        out[name] = w[name].astype(BF16)
    n = w['mla_w_in'].shape[0]
    pad = LANES - MLA_ROPE
    out['mla_w_in'] = jnp.pad(w['mla_w_in'], ((0, 0), (0, 0), (0, pad))).astype(BF16)
    wq = w['mla_w_qb'].reshape(n, MLA_Q_RANK, MLA_HEADS, MLA_NOPE + MLA_ROPE)
    out['mla_w_qb'] = jnp.pad(wq, ((0, 0), (0, 0), (0, 0), (0, pad))).reshape(n, MLA_Q_RANK, -1).astype(BF16)
    gpad = jnp.pad(w['mla_qk_norm'].astype(F32), ((0, 0), (0, 0), (0, pad)))
    out['mla_gq'] = gpad[:, 0:1] * ((MLA_NOPE + MLA_ROPE) ** -0.5)
    out['mla_gk'] = gpad[:, 1:2]
    g = w['gqa_qk_norm'].astype(F32)
    out['gqa_gains'] = jnp.concatenate([jnp.tile(g[:, 0] * (GQA_HD ** -0.5), (1, GQA_HEADS)),
                                        jnp.tile(g[:, 1], (1, GQA_KV_HEADS))], axis=-1)[:, None, :]
    return out


def kernel(x_prompt, x_sample, mem_prompt, mem_sample, norm_mix, norm_mem, norm_memtok, norm_mlp, ret_w_in, ret_decay, ret_out_norm, ret_w_out, hg_w_in, hg_lb, hg_out_norm, hg_w_out, mla_w_in, mla_q_norm, mla_kv_norm, mla_w_qb, mla_w_kvb, mla_qk_norm, mla_w_out, gqa_w_in, gqa_qk_norm, gqa_w_out, mem_w_q, mem_w_kv, mem_qk_norm, mem_w_out, mlp_w1, mlp_w2):
    w = _prepare_weights(dict(
        norm_mix=norm_mix, norm_mem=norm_mem, norm_memtok=norm_memtok, norm_mlp=norm_mlp,
        ret_w_in=ret_w_in, ret_decay=ret_decay, ret_out_norm=ret_out_norm, ret_w_out=ret_w_out,
        hg_w_in=hg_w_in, hg_lb=hg_lb, hg_out_norm=hg_out_norm, hg_w_out=hg_w_out,
        mla_w_in=mla_w_in, mla_q_norm=mla_q_norm, mla_kv_norm=mla_kv_norm, mla_w_qb=mla_w_qb,
        mla_w_kvb=mla_w_kvb, mla_qk_norm=mla_qk_norm, mla_w_out=mla_w_out,
        gqa_w_in=gqa_w_in, gqa_qk_norm=gqa_qk_norm, gqa_w_out=gqa_w_out,
        mem_w_q=mem_w_q, mem_w_kv=mem_w_kv, mem_qk_norm=mem_qk_norm, mem_w_out=mem_w_out,
        mlp_w1=mlp_w1, mlp_w2=mlp_w2))
    nb = x_prompt.shape[0]
    x = jnp.concatenate([x_prompt, x_sample], axis=0)
    mem = jnp.concatenate([mem_prompt, mem_sample], axis=0)
    y = _trunk(x, mem, w)
    return (y[:nb], y[nb:])
```

```python
import functools

import jax
import jax.numpy as jnp
from jax import lax
from jax.experimental import pallas as pl
from jax.experimental.pallas import tpu as pltpu

F32 = jnp.float32
BF16 = jnp.bfloat16

D_MODEL = 2048
GRID_W = 64
NORM_EPS = 1e-6
RET_HEADS, RET_DK, RET_DV = 8, 256, 512
RET_ROPE_BASE = 10000.0
HG_HEADS, HG_DK, HG_DV, HG_CHUNK = 16, 128, 128, 32
MLA_HEADS, MLA_Q_RANK, MLA_KV_RANK, MLA_NOPE, MLA_ROPE, MLA_V = 16, 512, 512, 128, 64, 128
MLA_ROPE_BASE = 10000.0
GQA_HEADS, GQA_KV_HEADS, GQA_HD = 16, 4, 128
GQA_ROPE_BASE = 10000.0
MEM_HEADS, MEM_HD = 4, 128

LANES = 128
VMEM_LIMIT_BYTES = 56 * 2 ** 20

HG_BLOCK = 128
HG_HEADS_PER_STEP = 4
MLA_HEADS_PER_STEP = 4


def _params(sem):
    return pltpu.CompilerParams(dimension_semantics=sem, vmem_limit_bytes=VMEM_LIMIT_BYTES)


def _tile(n, pref):
    return pref if n % pref == 0 else n


def _rms_scale(x, width=None):
    width = x.shape[-1] if width is None else width
    ms = jnp.sum(x * x, axis=-1, keepdims=True) * (1.0 / width)
    return x * lax.rsqrt(ms + NORM_EPS)


def _sigmoid(x):
    return 1.0 / (1.0 + jnp.exp(-x))


def _dot(a, b):
    return jnp.dot(a, b, preferred_element_type=F32)


def _dot_nt(a, b):
    return lax.dot_general(a, b, (((1,), (1,)), ((), ())), preferred_element_type=F32)


def _rope_pairs32(x, c, s_lo, s_hi):
    return x * c + pltpu.roll(x, 32, 1) * s_hi + pltpu.roll(x, 96, 1) * s_lo


def _norm_matmul_kernel(x_ref, g_ref, w_ref, o_ref, xn_ref):
    @pl.when(pl.program_id(1) == 0)
    def _():
        x = x_ref[...].astype(F32)
        xn_ref[...] = (_rms_scale(x) * g_ref[...]).astype(BF16)

    o_ref[...] = _dot(xn_ref[...], w_ref[...]).astype(o_ref.dtype)


def _norm_matmul(x, g, w, out_dtype, tm=1024, tn=1024):
    T, K = x.shape
    N = w.shape[1]
    tm, tn = _tile(T, tm), _tile(N, tn)
    return pl.pallas_call(
        _norm_matmul_kernel,
        grid=(T // tm, N // tn),
        in_specs=[pl.BlockSpec((tm, K), lambda i, j: (i, 0)),
                  pl.BlockSpec((1, K), lambda i, j: (0, 0)),
                  pl.BlockSpec((K, tn), lambda i, j: (0, j))],
        out_specs=pl.BlockSpec((tm, tn), lambda i, j: (i, j)),
        out_shape=jax.ShapeDtypeStruct((T, N), out_dtype),
        scratch_shapes=[pltpu.VMEM((tm, K), BF16)],
        compiler_params=_params(("parallel", "arbitrary")),
        name="norm_matmul",
    )(x, g.reshape(1, K).astype(F32), w)


def _matmul_residual_kernel(a_ref, w_ref, x_ref, o_ref):
    o_ref[...] = x_ref[...] + _dot(a_ref[...], w_ref[...])


def _matmul_residual(a, w, x, tm=1024, tn=512):
    T, K = a.shape
    N = w.shape[1]
    tm, tn = _tile(T, tm), _tile(N, tn)
    return pl.pallas_call(
        _matmul_residual_kernel,
        grid=(T // tm, N // tn),
        in_specs=[pl.BlockSpec((tm, K), lambda i, j: (i, 0)),
                  pl.BlockSpec((K, tn), lambda i, j: (0, j)),
                  pl.BlockSpec((tm, tn), lambda i, j: (i, j))],
        out_specs=pl.BlockSpec((tm, tn), lambda i, j: (i, j)),
        out_shape=jax.ShapeDtypeStruct((T, N), F32),
        input_output_aliases={2: 0},
        compiler_params=_params(("parallel", "arbitrary")),
        name="matmul_residual",
    )(a, w, x)


def _mlp_kernel(x_ref, g_ref, w1_ref, w2_ref, o_ref, xn_ref):
    @pl.when(pl.program_id(1) == 0)
    def _():
        x = x_ref[...]
        xn_ref[...] = (_rms_scale(x) * g_ref[...]).astype(BF16)
        o_ref[...] = x

    a = jnp.maximum(_dot(xn_ref[...], w1_ref[...]), 0.0)
    o_ref[...] += _dot((a * a).astype(BF16), w2_ref[...])


def _mlp(x, g, w1, w2, row0=0, rows=None, tm=512, tf=1024):
    D = x.shape[1]
    rows = x.shape[0] if rows is None else rows
    Fd = w1.shape[1]
    tm, tf = _tile(rows, tm), _tile(Fd, tf)
    assert row0 % tm == 0
    blk0 = row0 // tm
    return pl.pallas_call(
        _mlp_kernel,
        grid=(rows // tm, Fd // tf),
        in_specs=[pl.BlockSpec((tm, D), lambda i, f: (blk0 + i, 0)),
                  pl.BlockSpec((1, D), lambda i, f: (0, 0)),
                  pl.BlockSpec((D, tf), lambda i, f: (0, f)),
                  pl.BlockSpec((tf, D), lambda i, f: (f, 0))],
        out_specs=pl.BlockSpec((tm, D), lambda i, f: (i, 0)),
        out_shape=jax.ShapeDtypeStruct((rows, D), F32),
        scratch_shapes=[pltpu.VMEM((tm, D), BF16)],
        compiler_params=_params(("parallel", "arbitrary")),
        name="mlp",
    )(x, g.reshape(1, D).astype(F32), w1, w2)


def _xattn_kernel(x_ref, g_ref, wq_ref, kv_ref, gq_ref, gk_ref, wo_ref, o_ref):
    x = x_ref[0]
    xn = (_rms_scale(x) * g_ref[...]).astype(BF16)
    q = _dot(xn, wq_ref[...])
    kv = kv_ref[0].astype(F32)
    width = MEM_HEADS * MEM_HD
    outs = []
    for h in range(MEM_HEADS):
        cs = slice(h * MEM_HD, (h + 1) * MEM_HD)
        qh = (_rms_scale(q[:, cs]) * gq_ref[...]).astype(BF16)
        kh = (_rms_scale(kv[:, cs]) * gk_ref[...]).astype(BF16)
        vh = kv_ref[0, :, width + h * MEM_HD: width + (h + 1) * MEM_HD]
        s = _dot_nt(qh, kh)
        p = jnp.exp(s - jnp.max(s, axis=-1, keepdims=True))
        l = jnp.sum(p, axis=-1, keepdims=True)
        outs.append((_dot(p.astype(BF16), vh) / l).astype(BF16))
    o = jnp.concatenate(outs, axis=-1)
    o_ref[0] = x + _dot(o, wo_ref[...])


def _xattn(x, kv, g, wq, gq, gk, wo, tm=512):
    B, L, D = x.shape
    M = kv.shape[1]
    width = MEM_HEADS * MEM_HD
    tm = _tile(L, tm)
    return pl.pallas_call(
        _xattn_kernel,
        grid=(B, L // tm),
        in_specs=[pl.BlockSpec((1, tm, D), lambda b, i: (b, i, 0)),
                  pl.BlockSpec((1, D), lambda b, i: (0, 0)),
                  pl.BlockSpec((D, width), lambda b, i: (0, 0)),
                  pl.BlockSpec((1, M, 2 * width), lambda b, i: (b, 0, 0)),
                  pl.BlockSpec((1, MEM_HD), lambda b, i: (0, 0)),
                  pl.BlockSpec((1, MEM_HD), lambda b, i: (0, 0)),
                  pl.BlockSpec((width, D), lambda b, i: (0, 0))],
        out_specs=pl.BlockSpec((1, tm, D), lambda b, i: (b, i, 0)),
        out_shape=jax.ShapeDtypeStruct((B, L, D), F32),
        input_output_aliases={0: 0},
        compiler_params=_params(("parallel", "arbitrary")),
        name="mem_xattn",
    )(x, g.reshape(1, D).astype(F32), wq, kv,
      (gq.astype(F32) * (MEM_HD ** -0.5)).reshape(1, MEM_HD), gk.reshape(1, MEM_HD).astype(F32), wo)


def _attn_kernel(q_ref, k_ref, v_ref, o_ref, *, heads, shared_kv, dq, dv):
    for g in range(heads):
        kg = 0 if shared_kv else g
        q = q_ref[0, :, g * dq:(g + 1) * dq]
        k = k_ref[0, :, kg * dq:(kg + 1) * dq]
        v = v_ref[0, :, kg * dv:(kg + 1) * dv]
        s = _dot_nt(q, k)
        p = jnp.exp(s - jnp.max(s, axis=-1, keepdims=True))
        l = jnp.sum(p, axis=-1, keepdims=True)
        o_ref[0, :, g * dv:(g + 1) * dv] = (_dot(p.astype(BF16), v) / l).astype(o_ref.dtype)


def _attention(q, k, v, *, steps, heads, shared_kv, dq, dv, k_block0, v_block0, tq=512):
    B, L = q.shape[0], q.shape[1]
    tq = _tile(L, tq)
    kvh = 1 if shared_kv else heads
    return pl.pallas_call(
        functools.partial(_attn_kernel, heads=heads, shared_kv=shared_kv, dq=dq, dv=dv),
        grid=(B, steps, L // tq),
        in_specs=[pl.BlockSpec((1, tq, heads * dq), lambda b, h, i: (b, i, h)),
                  pl.BlockSpec((1, L, kvh * dq), lambda b, h, i: (b, 0, k_block0 + h)),
                  pl.BlockSpec((1, L, kvh * dv), lambda b, h, i: (b, 0, v_block0 + h))],
        out_specs=pl.BlockSpec((1, tq, heads * dv), lambda b, h, i: (b, i, h)),
        out_shape=jax.ShapeDtypeStruct((B, L, steps * heads * dv), BF16),
        compiler_params=_params(("parallel", "parallel", "arbitrary")),
        name="softmax_attention",
    )(q, k, v)


def _retention_kernel(q_ref, k_ref, v_ref, g_ref, cos_ref, sin_ref, dec_ref, on_ref, o_ref, *, tq):
    L = q_ref.shape[1]
    half = RET_DK // 2
    cos = cos_ref[...]
    sin = sin_ref[...]

    def rope(ref, scale):
        x1 = ref[0, :, :half].astype(F32)
        x2 = ref[0, :, half:].astype(F32)
        y = jnp.concatenate([x1 * cos - x2 * sin, x1 * sin + x2 * cos], axis=-1)
        return (y * scale).astype(BF16)

    q = rope(q_ref, RET_DK ** -0.5)
    k = rope(k_ref, 1.0)
    v = v_ref[0]
    dec = dec_ref[0]
    lg = jnp.minimum(dec, 0.0) - jnp.log1p(jnp.exp(-jnp.abs(dec)))
    lg_f = lg[0:1, 0:1]
    lg_b = lg[1:2, 0:1]
    for i in range(L // tq):
        rows = slice(i * tq, (i + 1) * tq)
        s = _dot_nt(q[rows], k)
        t = lax.broadcasted_iota(jnp.int32, (tq, L), 0) + i * tq
        diff = (t - lax.broadcasted_iota(jnp.int32, (tq, L), 1)).astype(F32)
        decay = jnp.exp(jnp.where(diff >= 0, diff * lg_f, -diff * lg_b))
        decay = jnp.where(diff == 0, 2.0, decay)
        o = _dot((s * decay).astype(BF16), v)
        y = _rms_scale(o) * on_ref[...]
        gate = g_ref[0, rows, :].astype(F32)
        o_ref[0, rows, :] = (y * (gate * _sigmoid(gate))).astype(o_ref.dtype)


def _retention(p, cos, sin, dec, out_norm, tq=512):
    B, L, _ = p.shape
    H = RET_HEADS
    tq = _tile(L, tq)
    v_block0 = 2 * H * RET_DK // RET_DV
    return pl.pallas_call(
        functools.partial(_retention_kernel, tq=tq),
        grid=(H, B),
        in_specs=[pl.BlockSpec((1, L, RET_DK), lambda h, b: (b, 0, h)),
                  pl.BlockSpec((1, L, RET_DK), lambda h, b: (b, 0, H + h)),
                  pl.BlockSpec((1, L, RET_DV), lambda h, b: (b, 0, v_block0 + h)),
                  pl.BlockSpec((1, L, RET_DV), lambda h, b: (b, 0, v_block0 + H + h)),
                  pl.BlockSpec((L, RET_DK // 2), lambda h, b: (0, 0)),
                  pl.BlockSpec((L, RET_DK // 2), lambda h, b: (0, 0)),
                  pl.BlockSpec((1, 2, LANES), lambda h, b: (h, 0, 0)),
                  pl.BlockSpec((1, RET_DV), lambda h, b: (0, 0))],
        out_specs=pl.BlockSpec((1, L, RET_DV), lambda h, b: (b, 0, h)),
        out_shape=jax.ShapeDtypeStruct((B, L, H * RET_DV), BF16),
        compiler_params=_params(("parallel", "arbitrary")),
        name="retention",
    )(p, p, p, p, cos, sin, dec, out_norm.reshape(1, RET_DV).astype(F32))


def _hgrn_kernel(q_ref, ff_ref, fb_ref, i_ref, g_ref, lb_ref, on_ref, m_ref, o_ref,
                 of_ref, ob_ref, st_ref, *, heads):
    L = q_ref.shape[1]
    nblk = L // HG_BLOCK
    n_chunks = HG_BLOCK // HG_CHUNK
    width = heads * HG_DK
    rows = lax.broadcasted_iota(jnp.int32, (HG_BLOCK, HG_BLOCK), 0)
    cols = lax.broadcasted_iota(jnp.int32, (HG_BLOCK, HG_BLOCK), 1)
    same_chunk = (rows // HG_CHUNK) == (cols // HG_CHUNK)
    masks = (same_chunk & (cols <= rows), same_chunk & (cols >= rows))
    row_chunk = lax.broadcasted_iota(jnp.int32, (HG_BLOCK, HG_DK), 0) // HG_CHUNK
    st_ref[...] = jnp.zeros_like(st_ref)

    def prep(r, d):
        rs = pl.ds(pl.multiple_of(r * HG_BLOCK, HG_BLOCK), HG_BLOCK)
        fz = (ff_ref if d == 0 else fb_ref)[0, rs, :].astype(F32)
        lb = lb_ref[...]
        f = lb + (1.0 - lb) * _sigmoid(fz)
        lf = jnp.log(f)
        kk = 1.0 - f
        hi = lf.astype(BF16)
        lo = (lf - hi.astype(F32)).astype(BF16)
        cb = _dot(m_ref[d], jnp.concatenate([hi, lo], axis=1))
        b = cb[:, :width] + cb[:, width:]
        last = [c * HG_CHUNK + (HG_CHUNK - 1 if d == 0 else 0) for c in range(n_chunks)]
        bl_rows = [b[i:i + 1, :] for i in last]
        bl = jnp.concatenate([jnp.broadcast_to(row, (HG_CHUNK, width)) for row in bl_rows], axis=0)
        v = i_ref[0, rs, :]
        qt = (q_ref[0, rs, :].astype(F32) * jnp.exp(b)).astype(BF16)
        kt = (kk * jnp.exp(-b)).astype(BF16)
        ks = kk * jnp.exp(bl - b)
        dec = [jnp.exp(row) for row in bl_rows]
        return rs, v, qt, kt, ks, dec

    def body(t, carry):
        preps = (prep(t, 0), prep(nblk - 1 - t, 1))
        items = [(h, d) for h in range(heads) for d in (0, 1)]
        col = lambda a, h: a[:, h * HG_DK:(h + 1) * HG_DK]
        v = {(h, d): col(preps[d][1], h) for h, d in items}
        qt = {(h, d): col(preps[d][2], h) for h, d in items}
        s = {it: _dot_nt(qt[it], col(preps[it[1]][3], it[0])) for it in items}
        s = {it: jnp.where(masks[it[1]], s[it], 0.0).astype(BF16) for it in items}
        o = {it: _dot(s[it], v[it]) for it in items}
        kv = {}
        for it in items:
            ks = col(preps[it[1]][4], it[0])
            ks_cols = jnp.concatenate([jnp.where(row_chunk == c, ks, 0.0) for c in range(n_chunks)], axis=1)
            kv[it] = _dot(v[it].astype(F32).T.astype(BF16), ks_cols.astype(BF16))
        st = {(h, d): st_ref[2 * h + d] for h, d in items}
        parts = {it: [None] * n_chunks for it in items}
        for step in range(n_chunks):
            for h, d in items:
                c = step if d == 0 else n_chunks - 1 - step
                cr = slice(c * HG_CHUNK, (c + 1) * HG_CHUNK)
                parts[h, d][c] = o[h, d][cr] + _dot_nt(qt[h, d][cr], st[h, d].astype(BF16))
                st[h, d] = st[h, d] * col(preps[d][5][c], h) + kv[h, d][:, c * HG_DK:(c + 1) * HG_DK]
        for h, d in items:
            st_ref[2 * h + d] = st[h, d]
        for d in (0, 1):
            out = jnp.concatenate([jnp.concatenate(parts[h, d], axis=0) for h in range(heads)], axis=1)
            (of_ref if d == 0 else ob_ref)[preps[d][0], :] = out
        return carry

    lax.fori_loop(0, nblk, body, 0)
    for h in range(heads):
        cs = slice(h * HG_DV, (h + 1) * HG_DV)
        y = _rms_scale(of_ref[:, cs] + ob_ref[:, cs]) * on_ref[...]
        gate = g_ref[0, :, cs].astype(F32)
        o_ref[0, :, cs] = (y * (gate * _sigmoid(gate))).astype(o_ref.dtype)


def _hgrn(p, lb, out_norm):
    B, L, _ = p.shape
    heads = HG_HEADS_PER_STEP
    nb = HG_HEADS // heads
    w = heads * HG_DK
    c = HG_CHUNK
    idx = jnp.arange(HG_BLOCK)
    same = (idx[:, None] // c) == (idx[None, :] // c)
    tri_f = same & (idx[None, :] <= idx[:, None])
    tri_b = same & (idx[None, :] >= idx[:, None])
    m = jnp.stack([tri_f, tri_b]).astype(BF16)
    spec = lambda off: pl.BlockSpec((1, L, w), lambda b, j: (b, 0, off * nb + j))
    return pl.pallas_call(
        functools.partial(_hgrn_kernel, heads=heads),
        grid=(B, nb),
        in_specs=[spec(0), spec(1), spec(2), spec(3), spec(4),
                  pl.BlockSpec((1, w), lambda b, j: (0, j)),
                  pl.BlockSpec((1, HG_DV), lambda b, j: (0, 0)),
                  pl.BlockSpec((2, HG_BLOCK, HG_BLOCK), lambda b, j: (0, 0, 0))],
        out_specs=pl.BlockSpec((1, L, w), lambda b, j: (b, 0, j)),
        out_shape=jax.ShapeDtypeStruct((B, L, HG_HEADS * HG_DV), BF16),
        scratch_shapes=[pltpu.VMEM((L, w), F32), pltpu.VMEM((L, w), F32),
                        pltpu.VMEM((2 * heads, HG_DV, HG_DK), F32)],
        compiler_params=_params(("parallel", "arbitrary")),
        name="hgrn2",
    )(p, p, p, p, p, lb.reshape(1, -1).astype(F32), out_norm.reshape(1, HG_DV).astype(F32), m)


def _mla_q_kernel(c_ref, gn_ref, w_ref, gh_ref, cos_ref, slo_ref, shi_ref, o_ref, xn_ref):
    @pl.when(pl.program_id(1) == 0)
    def _():
        xn_ref[...] = (_rms_scale(c_ref[...]) * gn_ref[...]).astype(BF16)

    hw = 2 * LANES
    y_all = _dot(xn_ref[...], w_ref[...])
    for g in range(MLA_HEADS_PER_STEP):
        y = _rms_scale(y_all[:, g * hw:(g + 1) * hw], MLA_NOPE + MLA_ROPE) * gh_ref[...]
        o_ref[:, g * hw:g * hw + MLA_NOPE] = y[:, :MLA_NOPE].astype(o_ref.dtype)
        o_ref[:, g * hw + MLA_NOPE:(g + 1) * hw] = _rope_pairs32(
            y[:, MLA_NOPE:], cos_ref[...], slo_ref[...], shi_ref[...]).astype(o_ref.dtype)


def _mla_kv_kernel(c_ref, kr_ref, gn_ref, w_ref, gh_ref, cos_ref, slo_ref, shi_ref, k_ref, v_ref, xn_ref):
    @pl.when(pl.program_id(1) == 0)
    def _():
        xn_ref[...] = (_rms_scale(c_ref[...]) * gn_ref[...]).astype(BF16)

    hw = 2 * LANES
    y_all = _dot(xn_ref[...], w_ref[...])
    kr = kr_ref[...]
    kr_ss = jnp.sum(kr * kr, axis=-1, keepdims=True)
    gh = gh_ref[...]
    for g in range(MLA_HEADS_PER_STEP):
        kn = y_all[:, g * hw:g * hw + MLA_NOPE]
        ms = (jnp.sum(kn * kn, axis=-1, keepdims=True) + kr_ss) * (1.0 / (MLA_NOPE + MLA_ROPE))
        inv = lax.rsqrt(ms + NORM_EPS)
        k_ref[:, g * hw:g * hw + MLA_NOPE] = (kn * inv * gh[:, :MLA_NOPE]).astype(k_ref.dtype)
        k_ref[:, g * hw + MLA_NOPE:(g + 1) * hw] = _rope_pairs32(
            kr * inv * gh[:, MLA_NOPE:], cos_ref[...], slo_ref[...], shi_ref[...]).astype(k_ref.dtype)
        v_ref[:, g * MLA_V:(g + 1) * MLA_V] = y_all[:, g * hw + MLA_NOPE:(g + 1) * hw].astype(v_ref.dtype)


def _mla_qkv(c, L, q_norm, kv_norm, wq, wkv, gq, gk, tabs, tm=1024):
    T = c.shape[0]
    H = MLA_HEADS
    hps = MLA_HEADS_PER_STEP
    tm = _tile(L, tm)
    lt = L // tm
    hw = 2 * LANES
    tab_spec = pl.BlockSpec((tm, LANES), lambda i, h: (i % lt, 0))
    row = lambda n: pl.BlockSpec((1, n), lambda i, h: (0, 0))
    q = pl.pallas_call(
        _mla_q_kernel,
        grid=(T // tm, H // hps),
        in_specs=[pl.BlockSpec((tm, MLA_Q_RANK), lambda i, h: (i, 0)), row(MLA_Q_RANK),
                  pl.BlockSpec((MLA_Q_RANK, hps * hw), lambda i, h: (0, h)), row(hw),
                  tab_spec, tab_spec, tab_spec],
        out_specs=pl.BlockSpec((tm, hps * hw), lambda i, h: (i, h)),
        out_shape=jax.ShapeDtypeStruct((T, H * hw), BF16),
        scratch_shapes=[pltpu.VMEM((tm, MLA_Q_RANK), BF16)],
        compiler_params=_params(("parallel", "arbitrary")),
        name="mla_q",
    )(c, q_norm.reshape(1, -1).astype(F32), wq, gq, *tabs)
    k, v = pl.pallas_call(
        _mla_kv_kernel,
        grid=(T // tm, H // hps),
        in_specs=[pl.BlockSpec((tm, MLA_KV_RANK), lambda i, h: (i, 1)),
                  pl.BlockSpec((tm, LANES), lambda i, h: (i, (MLA_Q_RANK + MLA_KV_RANK) // LANES)),
                  row(MLA_KV_RANK),
                  pl.BlockSpec((MLA_KV_RANK, hps * hw), lambda i, h: (0, h)), row(hw),
                  tab_spec, tab_spec, tab_spec],
        out_specs=[pl.BlockSpec((tm, hps * hw), lambda i, h: (i, h)),
                   pl.BlockSpec((tm, hps * MLA_V), lambda i, h: (i, h))],
        out_shape=[jax.ShapeDtypeStruct((T, H * hw), BF16), jax.ShapeDtypeStruct((T, H * MLA_V), BF16)],
        scratch_shapes=[pltpu.VMEM((tm, MLA_KV_RANK), BF16)],
        compiler_params=_params(("parallel", "arbitrary")),
        name="mla_kv",
    )(c, c, kv_norm.reshape(1, -1).astype(F32), wkv, gk, *tabs)
    return q, k, v


def _head_norm_rope_kernel(x_ref, g_ref, cos_ref, slo_ref, shi_ref, o_ref, *, n_heads):
    for h in range(n_heads):
        cs = slice(h * LANES, (h + 1) * LANES)
        y = _rms_scale(x_ref[:, cs].astype(F32)) * g_ref[:, cs]
        o_ref[:, cs] = _rope_pairs32(y, cos_ref[...], slo_ref[...], shi_ref[...]).astype(o_ref.dtype)


def _head_norm_rope(p, L, n_heads, gains, tabs, tm=512):
    T = p.shape[0]
    tm = _tile(L, tm)
    lt = L // tm
    w = n_heads * LANES
    tab_spec = pl.BlockSpec((tm, LANES), lambda i: (i % lt, 0))
    return pl.pallas_call(
        functools.partial(_head_norm_rope_kernel, n_heads=n_heads),
        grid=(T // tm,),
        in_specs=[pl.BlockSpec((tm, w), lambda i: (i, 0)),
                  pl.BlockSpec((1, w), lambda i: (0, 0)),
                  tab_spec, tab_spec, tab_spec],
        out_specs=pl.BlockSpec((tm, w), lambda i: (i, 0)),
        out_shape=jax.ShapeDtypeStruct((T, w), BF16),
        compiler_params=_params(("parallel",)),
        name="head_norm_rope",
    )(p, gains, *tabs)


def _rope_tables32(pos_a, pos_b, base):
    half = 32
    freqs = base ** (-jnp.arange(half, dtype=F32) / half)

    def cs(pos):
        ang = pos[:, None] * freqs[None, :]
        return jnp.cos(ang), jnp.sin(ang)

    ca, sa = cs(pos_a)
    z = jnp.zeros_like(ca)
    if pos_b is None:
        cb, sb = z, z
    else:
        cb, sb = cs(pos_b)
    cos = jnp.concatenate([ca, ca, cb, cb], axis=-1)
    s_lo = jnp.concatenate([-sa, z, -sb, z], axis=-1)
    s_hi = jnp.concatenate([z, sa, z, sb], axis=-1)
    return cos, s_lo, s_hi


def _trunk(x, mem, w, group_batches):
    B, L, D = x.shape
    T = B * L
    M = mem.shape[1]
    x = x.reshape(T, D)
    mem = mem.reshape(B * M, D)
    pos = jnp.arange(L, dtype=F32)
    sm = jax.nn.softmax(w['hg_lb'].astype(F32), axis=0)
    lb_all = jnp.cumsum(sm, axis=0) - sm[0]
    depth = w['norm_mix'].shape[0]
    for i in range(depth):
        kind, j = i % 4, i // 4
        if kind == 0:
            p = _norm_matmul(x, w['norm_mix'][i], w['ret_w_in'][j], BF16)
            half = RET_DK // 2
            ang = pos[:, None] * (RET_ROPE_BASE ** (-jnp.arange(half, dtype=F32) / half))[None, :]
            dec = jnp.broadcast_to(w['ret_decay'][j].astype(F32).T[:, :, None], (RET_HEADS, 2, LANES))
            o = _retention(p.reshape(B, L, -1), jnp.cos(ang), jnp.sin(ang), dec, w['ret_out_norm'][j])
            x = _matmul_residual(o.reshape(T, -1), w['ret_w_out'][j], x)
        elif kind == 1:
            p = _norm_matmul(x, w['norm_mix'][i], w['hg_w_in'][j], BF16)
            o = _hgrn(p.reshape(B, L, -1), lb_all[i], w['hg_out_norm'][j])
            x = _matmul_residual(o.reshape(T, -1), w['hg_w_out'][j], x, tm=512, tn=D)
        elif kind == 2:
            c = _norm_matmul(x, w['norm_mix'][i], w['mla_w_in'][j], F32)
            tabs = _rope_tables32(pos, None, MLA_ROPE_BASE)
            q, k, v = _mla_qkv(c, L, w['mla_q_norm'][j], w['mla_kv_norm'][j], w['mla_w_qb'][j], w['mla_w_kvb'][j],
                               w['mla_gq'][j], w['mla_gk'][j], tabs)
            o = _attention(q.reshape(B, L, -1), k.reshape(B, L, -1), v.reshape(B, L, -1),
                           steps=MLA_HEADS // MLA_HEADS_PER_STEP, heads=MLA_HEADS_PER_STEP, shared_kv=False,
                           dq=2 * LANES, dv=MLA_V, k_block0=0, v_block0=0)
            x = _matmul_residual(o.reshape(T, -1), w['mla_w_out'][j], x, tm=512, tn=D)
        else:
            p = _norm_matmul(x, w['norm_mix'][i], w['gqa_w_in'][j], BF16)
            t = jnp.arange(L)
            tabs = _rope_tables32((t // GRID_W).astype(F32), (t % GRID_W).astype(F32), GQA_ROPE_BASE)
            nqk = GQA_HEADS + GQA_KV_HEADS
            qk = _head_norm_rope(p, L, nqk, w['gqa_gains'][j], tabs)
            o = _attention(qk.reshape(B, L, -1), qk.reshape(B, L, -1), p.reshape(B, L, -1),
                           steps=GQA_KV_HEADS, heads=GQA_HEADS // GQA_KV_HEADS, shared_kv=True,
                           dq=GQA_HD, dv=GQA_HD, k_block0=GQA_HEADS, v_block0=nqk)
            x = _matmul_residual(o.reshape(T, -1), w['gqa_w_out'][j], x, tm=512, tn=D)
        kv = _norm_matmul(mem, w['norm_memtok'][i], w['mem_w_kv'][i], BF16)
        x = _xattn(x.reshape(B, L, D), kv.reshape(B, M, -1), w['norm_mem'][i], w['mem_w_q'][i],
                   w['mem_qk_norm'][i, 0], w['mem_qk_norm'][i, 1], w['mem_w_out'][i]).reshape(T, D)
        mlp_w = (w['norm_mlp'][i], w['mlp_w1'][i], w['mlp_w2'][i])
        if i < depth - 1:
            x = _mlp(x, *mlp_w)
    outs = []
    row0 = 0
    for nb in group_batches:
        outs.append(_mlp(x, *mlp_w, row0=row0, rows=nb * L).reshape(nb, L, D))
        row0 += nb * L
    return tuple(outs)


def _prepare_weights(w):
    out = dict(w)
    for name in ('ret_w_in', 'ret_w_out', 'hg_w_in', 'hg_w_out', 'mla_w_out', 'gqa_w_in', 'gqa_w_out',
                 'mem_w_q', 'mem_w_kv', 'mem_w_out', 'mlp_w1', 'mlp_w2', 'mla_w_kvb'):
        out[name] = w[name].astype(BF16)
    n = w['mla_w_in'].shape[0]
    pad = LANES - MLA_ROPE
    out['mla_w_in'] = jnp.pad(w['mla_w_in'], ((0, 0), (0, 0), (0, pad))).astype(BF16)
    wq = w['mla_w_qb'].reshape(n, MLA_Q_RANK, MLA_HEADS, MLA_NOPE + MLA_ROPE)
    out['mla_w_qb'] = jnp.pad(wq, ((0, 0), (0, 0), (0, 0), (0, pad))).reshape(n, MLA_Q_RANK, -1).astype(BF16)
    gpad = jnp.pad(w['mla_qk_norm'].astype(F32), ((0, 0), (0, 0), (0, pad)))
    out['mla_gq'] = gpad[:, 0:1] * ((MLA_NOPE + MLA_ROPE) ** -0.5)
    out['mla_gk'] = gpad[:, 1:2]
    g = w['gqa_qk_norm'].astype(F32)
    out['gqa_gains'] = jnp.concatenate([jnp.tile(g[:, 0] * (GQA_HD ** -0.5), (1, GQA_HEADS)),
                                        jnp.tile(g[:, 1], (1, GQA_KV_HEADS))], axis=-1)[:, None, :]
    return out


def kernel(x_prompt, x_sample, mem_prompt, mem_sample, norm_mix, norm_mem, norm_memtok, norm_mlp, ret_w_in, ret_decay, ret_out_norm, ret_w_out, hg_w_in, hg_lb, hg_out_norm, hg_w_out, mla_w_in, mla_q_norm, mla_kv_norm, mla_w_qb, mla_w_kvb, mla_qk_norm, mla_w_out, gqa_w_in, gqa_qk_norm, gqa_w_out, mem_w_q, mem_w_kv, mem_qk_norm, mem_w_out, mlp_w1, mlp_w2):
    w = _prepare_weights(dict(
        norm_mix=norm_mix, norm_mem=norm_mem, norm_memtok=norm_memtok, norm_mlp=norm_mlp,
        ret_w_in=ret_w_in, ret_decay=ret_decay, ret_out_norm=ret_out_norm, ret_w_out=ret_w_out,
        hg_w_in=hg_w_in, hg_lb=hg_lb, hg_out_norm=hg_out_norm, hg_w_out=hg_w_out,
        mla_w_in=mla_w_in, mla_q_norm=mla_q_norm, mla_kv_norm=mla_kv_norm, mla_w_qb=mla_w_qb,
        mla_w_kvb=mla_w_kvb, mla_qk_norm=mla_qk_norm, mla_w_out=mla_w_out,
        gqa_w_in=gqa_w_in, gqa_qk_norm=gqa_qk_norm, gqa_w_out=gqa_w_out,
        mem_w_q=mem_w_q, mem_w_kv=mem_w_kv, mem_qk_norm=mem_qk_norm, mem_w_out=mem_w_out,
        mlp_w1=mlp_w1, mlp_w2=mlp_w2))
    x = jnp.concatenate([x_prompt, x_sample], axis=0)
    mem = jnp.concatenate([mem_prompt, mem_sample], axis=0)
    return _trunk(x, mem, w, (x_prompt.shape[0], x_sample.shape[0]))
```

```python
import functools

import jax
import jax.numpy as jnp
from jax import lax
from jax.experimental import pallas as pl
from jax.experimental.pallas import tpu as pltpu

F32 = jnp.float32
BF16 = jnp.bfloat16

D_MODEL = 2048
GRID_W = 64
NORM_EPS = 1e-6
LOG2_E = 1.4426950408889634
RET_HEADS, RET_DK, RET_DV = 8, 256, 512
RET_ROPE_BASE = 10000.0
HG_HEADS, HG_DK, HG_DV, HG_CHUNK = 16, 128, 128, 32
MLA_HEADS, MLA_Q_RANK, MLA_KV_RANK, MLA_NOPE, MLA_ROPE, MLA_V = 16, 512, 512, 128, 64, 128
MLA_ROPE_BASE = 10000.0
GQA_HEADS, GQA_KV_HEADS, GQA_HD = 16, 4, 128
GQA_ROPE_BASE = 10000.0
MEM_HEADS, MEM_HD = 4, 128

LANES = 128
VMEM_LIMIT_BYTES = 56 * 2 ** 20

HG_BLOCK = 128
HG_HEADS_PER_STEP = 4
MLA_HEADS_PER_STEP = 4


def _params(sem):
    return pltpu.CompilerParams(dimension_semantics=sem, vmem_limit_bytes=VMEM_LIMIT_BYTES)


def _tile(n, pref):
    return pref if n % pref == 0 else n


def _rms_scale(x, width=None):
    width = x.shape[-1] if width is None else width
    ms = jnp.sum(x * x, axis=-1, keepdims=True) * (1.0 / width)
    return x * lax.rsqrt(ms + NORM_EPS)


def _inv_rms_lanes(x, width):
    sq = (x * x).astype(BF16)
    ss = _dot(sq, jnp.ones((x.shape[-1], LANES), BF16))
    return lax.rsqrt(ss * (1.0 / width) + NORM_EPS)


def _sigmoid(x):
    return 1.0 / (1.0 + jnp.exp(-x))


def _dot(a, b):
    return jnp.dot(a, b, preferred_element_type=F32)


def _dot_nt(a, b):
    return lax.dot_general(a, b, (((1,), (1,)), ((), ())), preferred_element_type=F32)


def _rope_pairs(x, c, s):
    return x * c + pltpu.roll(x, LANES // 2, 1) * s


def _norm_matmul_kernel(x_ref, g_ref, w_ref, o_ref, xn_ref):
    @pl.when(pl.program_id(1) == 0)
    def _():
        x = x_ref[...].astype(F32)
        xn_ref[...] = (_rms_scale(x) * g_ref[...]).astype(BF16)

    o_ref[...] = _dot(xn_ref[...], w_ref[...]).astype(o_ref.dtype)


def _norm_matmul(x, g, w, out_dtype, tm=1024, tn=1024):
    T, K = x.shape
    N = w.shape[1]
    tm, tn = _tile(T, tm), _tile(N, tn)
    return pl.pallas_call(
        _norm_matmul_kernel,
        grid=(T // tm, N // tn),
        in_specs=[pl.BlockSpec((tm, K), lambda i, j: (i, 0)),
                  pl.BlockSpec((1, K), lambda i, j: (0, 0)),
                  pl.BlockSpec((K, tn), lambda i, j: (0, j))],
        out_specs=pl.BlockSpec((tm, tn), lambda i, j: (i, j)),
        out_shape=jax.ShapeDtypeStruct((T, N), out_dtype),
        scratch_shapes=[pltpu.VMEM((tm, K), BF16)],
        compiler_params=_params(("parallel", "arbitrary")),
        name="norm_matmul",
    )(x, g.reshape(1, K).astype(F32), w)


def _matmul_residual_kernel(a_ref, w_ref, x_ref, o_ref):
    o_ref[...] = x_ref[...] + _dot(a_ref[...], w_ref[...])


def _matmul_residual(a, w, x, tm=1024, tn=512):
    T, K = a.shape
    N = w.shape[1]
    tm, tn = _tile(T, tm), _tile(N, tn)
    return pl.pallas_call(
        _matmul_residual_kernel,
        grid=(T // tm, N // tn),
        in_specs=[pl.BlockSpec((tm, K), lambda i, j: (i, 0)),
                  pl.BlockSpec((K, tn), lambda i, j: (0, j)),
                  pl.BlockSpec((tm, tn), lambda i, j: (i, j))],
        out_specs=pl.BlockSpec((tm, tn), lambda i, j: (i, j)),
        out_shape=jax.ShapeDtypeStruct((T, N), F32),
        input_output_aliases={2: 0},
        compiler_params=_params(("parallel", "arbitrary")),
        name="matmul_residual",
    )(a, w, x)


def _mlp_kernel(x_ref, g_ref, w1_ref, w2_ref, o_ref, xn_ref):
    @pl.when(pl.program_id(1) == 0)
    def _():
        x = x_ref[...]
        xn_ref[...] = (_rms_scale(x) * g_ref[...]).astype(BF16)
        o_ref[...] = x

    a = jnp.maximum(_dot(xn_ref[...], w1_ref[...]), 0.0)
    o_ref[...] += _dot((a * a).astype(BF16), w2_ref[...])


def _mlp(x, g, w1, w2, row0=0, rows=None, tm=512, tf=1024):
    D = x.shape[1]
    rows = x.shape[0] if rows is None else rows
    Fd = w1.shape[1]
    tm, tf = _tile(rows, tm), _tile(Fd, tf)
    assert row0 % tm == 0
    blk0 = row0 // tm
    return pl.pallas_call(
        _mlp_kernel,
        grid=(rows // tm, Fd // tf),
        in_specs=[pl.BlockSpec((tm, D), lambda i, f: (blk0 + i, 0)),
                  pl.BlockSpec((1, D), lambda i, f: (0, 0)),
                  pl.BlockSpec((D, tf), lambda i, f: (0, f)),
                  pl.BlockSpec((tf, D), lambda i, f: (f, 0))],
        out_specs=pl.BlockSpec((tm, D), lambda i, f: (i, 0)),
        out_shape=jax.ShapeDtypeStruct((rows, D), F32),
        scratch_shapes=[pltpu.VMEM((tm, D), BF16)],
        compiler_params=_params(("parallel", "arbitrary")),
        name="mlp",
    )(x, g.reshape(1, D).astype(F32), w1, w2)


def _xattn_kernel(x_ref, g_ref, wq_ref, kv_ref, gq_ref, gk_ref, wo_ref, o_ref):
    x = x_ref[0]
    xn = (_rms_scale(x) * g_ref[...]).astype(BF16)
    q = _dot(xn, wq_ref[...])
    kv = kv_ref[0].astype(F32)
    width = MEM_HEADS * MEM_HD
    outs = []
    for h in range(MEM_HEADS):
        cs = slice(h * MEM_HD, (h + 1) * MEM_HD)
        qh = (_rms_scale(q[:, cs]) * gq_ref[...]).astype(BF16)
        kh = (_rms_scale(kv[:, cs]) * gk_ref[...]).astype(BF16)
        vh = kv_ref[0, :, width + h * MEM_HD: width + (h + 1) * MEM_HD]
        s = _dot_nt(qh, kh)
        p = jnp.exp(s - jnp.max(s, axis=-1, keepdims=True))
        l = jnp.sum(p, axis=-1, keepdims=True)
        outs.append((_dot(p.astype(BF16), vh) / l).astype(BF16))
    o = jnp.concatenate(outs, axis=-1)
    o_ref[0] = x + _dot(o, wo_ref[...])


def _xattn(x, kv, g, wq, gq, gk, wo, tm=512):
    B, L, D = x.shape
    M = kv.shape[1]
    width = MEM_HEADS * MEM_HD
    tm = _tile(L, tm)
    return pl.pallas_call(
        _xattn_kernel,
        grid=(B, L // tm),
        in_specs=[pl.BlockSpec((1, tm, D), lambda b, i: (b, i, 0)),
                  pl.BlockSpec((1, D), lambda b, i: (0, 0)),
                  pl.BlockSpec((D, width), lambda b, i: (0, 0)),
                  pl.BlockSpec((1, M, 2 * width), lambda b, i: (b, 0, 0)),
                  pl.BlockSpec((1, MEM_HD), lambda b, i: (0, 0)),
                  pl.BlockSpec((1, MEM_HD), lambda b, i: (0, 0)),
                  pl.BlockSpec((width, D), lambda b, i: (0, 0))],
        out_specs=pl.BlockSpec((1, tm, D), lambda b, i: (b, i, 0)),
        out_shape=jax.ShapeDtypeStruct((B, L, D), F32),
        input_output_aliases={0: 0},
        compiler_params=_params(("parallel", "arbitrary")),
        name="mem_xattn",
    )(x, g.reshape(1, D).astype(F32), wq, kv,
      (gq.astype(F32) * (MEM_HD ** -0.5)).reshape(1, MEM_HD), gk.reshape(1, MEM_HD).astype(F32), wo)


def _attn_kernel(q_ref, k_ref, v_ref, o_ref, *, heads, shared_kv, dq, dv):
    def scores(g):
        kg = 0 if shared_kv else g
        return _dot_nt(q_ref[0, :, g * dq:(g + 1) * dq], k_ref[0, :, kg * dq:(kg + 1) * dq])

    s_next = scores(0)
    for g in range(heads):
        s, s_next = s_next, (scores(g + 1) if g + 1 < heads else None)
        kg = 0 if shared_kv else g
        v = v_ref[0, :, kg * dv:(kg + 1) * dv]
        p = jnp.exp2(s - jnp.max(s, axis=-1, keepdims=True))
        l = jnp.sum(p, axis=-1, keepdims=True)
        o_ref[0, :, g * dv:(g + 1) * dv] = (_dot(p.astype(BF16), v) / l).astype(o_ref.dtype)


def _attention(q, k, v, *, steps, heads, shared_kv, dq, dv, k_block0, v_block0, tq=512):
    B, L = q.shape[0], q.shape[1]
    tq = _tile(L, tq)
    kvh = 1 if shared_kv else heads
    return pl.pallas_call(
        functools.partial(_attn_kernel, heads=heads, shared_kv=shared_kv, dq=dq, dv=dv),
        grid=(B, steps, L // tq),
        in_specs=[pl.BlockSpec((1, tq, heads * dq), lambda b, h, i: (b, i, h)),
                  pl.BlockSpec((1, L, kvh * dq), lambda b, h, i: (b, 0, k_block0 + h)),
                  pl.BlockSpec((1, L, kvh * dv), lambda b, h, i: (b, 0, v_block0 + h))],
        out_specs=pl.BlockSpec((1, tq, heads * dv), lambda b, h, i: (b, i, h)),
        out_shape=jax.ShapeDtypeStruct((B, L, steps * heads * dv), BF16),
        compiler_params=_params(("parallel", "parallel", "arbitrary")),
        name="softmax_attention",
    )(q, k, v)


def _retention_kernel(q_ref, k_ref, v_ref, g_ref, cos_ref, sin_ref, dec_ref, on_ref, o_ref, *, tq):
    L = q_ref.shape[1]
    half = RET_DK // 2
    cos = cos_ref[...]
    sin = sin_ref[...]

    def rope(ref, scale):
        x1 = ref[0, :, :half].astype(F32)
        x2 = ref[0, :, half:].astype(F32)
        y = jnp.concatenate([x1 * cos - x2 * sin, x1 * sin + x2 * cos], axis=-1)
        return (y * scale).astype(BF16)

    q = rope(q_ref, RET_DK ** -0.5)
    k = rope(k_ref, 1.0)
    v = v_ref[0]
    dec = dec_ref[0]
    lg = jnp.minimum(dec, 0.0) - jnp.log1p(jnp.exp(-jnp.abs(dec)))
    lg_f = lg[0:1, 0:1]
    lg_b = lg[1:2, 0:1]
    n = L // tq
    ri = lax.broadcasted_iota(jnp.int32, (tq, tq), 0)
    diff = (ri - lax.broadcasted_iota(jnp.int32, (tq, tq), 1)).astype(F32)
    d_intra = jnp.where(diff == 0, 2.0, jnp.exp(jnp.where(diff >= 0, diff * lg_f, -diff * lg_b)))
    r = lax.broadcasted_iota(jnp.int32, (tq, RET_DK), 0).astype(F32)
    qdec = (jnp.exp((r + 1.0) * lg_f), jnp.exp((tq - r) * lg_b))
    kdec = (jnp.exp((tq - 1.0 - r) * lg_f), jnp.exp(r * lg_b))
    gc = (jnp.exp(tq * lg_f), jnp.exp(tq * lg_b))
    rows = [slice(i * tq, (i + 1) * tq) for i in range(n)]

    def kv_state(m, d):
        kd = (k[rows[m]].astype(F32) * kdec[d]).T.astype(BF16)
        return _dot(kd, v[rows[m]])

    state = [[None] * n, [None] * n]
    for i in range(1, n):
        prev = state[0][i - 1]
        state[0][i] = kv_state(i - 1, 0) if prev is None else prev * gc[0] + kv_state(i - 1, 0)
    for i in range(n - 2, -1, -1):
        nxt = state[1][i + 1]
        state[1][i] = kv_state(i + 1, 1) if nxt is None else nxt * gc[1] + kv_state(i + 1, 1)
    for i in range(n):
        qi = q[rows[i]]
        s = _dot_nt(qi, k[rows[i]]) * d_intra
        o = _dot(s.astype(BF16), v[rows[i]])
        for d in (0, 1):
            if state[d][i] is not None:
                o = o + _dot((qi.astype(F32) * qdec[d]).astype(BF16), state[d][i].astype(BF16))
        y = _rms_scale(o) * on_ref[...]
        gate = g_ref[0, rows[i], :].astype(F32)
        o_ref[0, rows[i], :] = (y * (gate * _sigmoid(gate))).astype(o_ref.dtype)


def _retention(p, cos, sin, dec, out_norm, tq=512):
    B, L, _ = p.shape
    H = RET_HEADS
    tq = _tile(L, tq)
    v_block0 = 2 * H * RET_DK // RET_DV
    return pl.pallas_call(
        functools.partial(_retention_kernel, tq=tq),
        grid=(H, B),
        in_specs=[pl.BlockSpec((1, L, RET_DK), lambda h, b: (b, 0, h)),
                  pl.BlockSpec((1, L, RET_DK), lambda h, b: (b, 0, H + h)),
                  pl.BlockSpec((1, L, RET_DV), lambda h, b: (b, 0, v_block0 + h)),
                  pl.BlockSpec((1, L, RET_DV), lambda h, b: (b, 0, v_block0 + H + h)),
                  pl.BlockSpec((L, RET_DK // 2), lambda h, b: (0, 0)),
                  pl.BlockSpec((L, RET_DK // 2), lambda h, b: (0, 0)),
                  pl.BlockSpec((1, 2, LANES), lambda h, b: (h, 0, 0)),
                  pl.BlockSpec((1, RET_DV), lambda h, b: (0, 0))],
        out_specs=pl.BlockSpec((1, L, RET_DV), lambda h, b: (b, 0, h)),
        out_shape=jax.ShapeDtypeStruct((B, L, H * RET_DV), BF16),
        compiler_params=_params(("parallel", "arbitrary")),
        name="retention",
    )(p, p, p, p, cos, sin, dec, out_norm.reshape(1, RET_DV).astype(F32))


def _hgrn_kernel(q_ref, ff_ref, fb_ref, i_ref, g_ref, lb_ref, on_ref, m_ref, o_ref,
                 of_ref, ob_ref, st_ref, *, heads):
    L = q_ref.shape[1]
    nblk = L // HG_BLOCK
    n_chunks = HG_BLOCK // HG_CHUNK
    width = heads * HG_DK
    rows = lax.broadcasted_iota(jnp.int32, (HG_BLOCK, HG_BLOCK), 0)
    cols = lax.broadcasted_iota(jnp.int32, (HG_BLOCK, HG_BLOCK), 1)
    same_chunk = (rows // HG_CHUNK) == (cols // HG_CHUNK)
    masks = (same_chunk & (cols <= rows), same_chunk & (cols >= rows))
    row_chunk = lax.broadcasted_iota(jnp.int32, (HG_BLOCK, HG_DK), 0) // HG_CHUNK
    st_ref[...] = jnp.zeros_like(st_ref)

    def prep(r, d):
        rs = pl.ds(pl.multiple_of(r * HG_BLOCK, HG_BLOCK), HG_BLOCK)
        fz = (ff_ref if d == 0 else fb_ref)[0, rs, :].astype(F32)
        lb = lb_ref[...]
        f = lb + (1.0 - lb) * _sigmoid(fz)
        lf = jnp.log(f)
        kk = 1.0 - f
        hi = lf.astype(BF16)
        lo = (lf - hi.astype(F32)).astype(BF16)
        cb = _dot(m_ref[d], jnp.concatenate([hi, lo], axis=1))
        b = cb[:, :width] + cb[:, width:]
        last = [c * HG_CHUNK + (HG_CHUNK - 1 if d == 0 else 0) for c in range(n_chunks)]
        bl_rows = [b[i:i + 1, :] for i in last]
        bl = jnp.concatenate([jnp.broadcast_to(row, (HG_CHUNK, width)) for row in bl_rows], axis=0)
        v = i_ref[0, rs, :]
        qt = (q_ref[0, rs, :].astype(F32) * jnp.exp(b)).astype(BF16)
        kt = (kk * jnp.exp(-b)).astype(BF16)
        ks = kk * jnp.exp(bl - b)
        dec = [jnp.exp(row) for row in bl_rows]
        return rs, v, qt, kt, ks, dec

    def body(t, carry):
        preps = (prep(t, 0), prep(nblk - 1 - t, 1))
        items = [(h, d) for h in range(heads) for d in (0, 1)]
        col = lambda a, h: a[:, h * HG_DK:(h + 1) * HG_DK]
        v = {(h, d): col(preps[d][1], h) for h, d in items}
        qt = {(h, d): col(preps[d][2], h) for h, d in items}
        s = {it: _dot_nt(qt[it], col(preps[it[1]][3], it[0])) for it in items}
        s = {it: jnp.where(masks[it[1]], s[it], 0.0).astype(BF16) for it in items}
        o = {it: _dot(s[it], v[it]) for it in items}
        kv = {}
        for it in items:
            ks = col(preps[it[1]][4], it[0])
            ks_cols = jnp.concatenate([jnp.where(row_chunk == c, ks, 0.0) for c in range(n_chunks)], axis=1)
            kv[it] = _dot(v[it].astype(F32).T.astype(BF16), ks_cols.astype(BF16))
        st = {(h, d): st_ref[2 * h + d] for h, d in items}
        parts = {it: [None] * n_chunks for it in items}
        for step in range(n_chunks):
            for h, d in items:
                c = step if d == 0 else n_chunks - 1 - step
                cr = slice(c * HG_CHUNK, (c + 1) * HG_CHUNK)
                parts[h, d][c] = o[h, d][cr] + _dot_nt(qt[h, d][cr], st[h, d].astype(BF16))
                st[h, d] = st[h, d] * col(preps[d][5][c], h) + kv[h, d][:, c * HG_DK:(c + 1) * HG_DK]
        for h, d in items:
            st_ref[2 * h + d] = st[h, d]
        for d in (0, 1):
            out = jnp.concatenate([jnp.concatenate(parts[h, d], axis=0) for h in range(heads)], axis=1)
            (of_ref if d == 0 else ob_ref)[preps[d][0], :] = out
        return carry

    lax.fori_loop(0, nblk, body, 0)
    for h in range(heads):
        cs = slice(h * HG_DV, (h + 1) * HG_DV)
        y = _rms_scale(of_ref[:, cs] + ob_ref[:, cs]) * on_ref[...]
        gate = g_ref[0, :, cs].astype(F32)
        o_ref[0, :, cs] = (y * (gate * _sigmoid(gate))).astype(o_ref.dtype)


def _hgrn(p, lb, out_norm):
    B, L, _ = p.shape
    heads = HG_HEADS_PER_STEP
    nb = HG_HEADS // heads
    w = heads * HG_DK
    c = HG_CHUNK
    idx = jnp.arange(HG_BLOCK)
    same = (idx[:, None] // c) == (idx[None, :] // c)
    tri_f = same & (idx[None, :] <= idx[:, None])
    tri_b = same & (idx[None, :] >= idx[:, None])
    m = jnp.stack([tri_f, tri_b]).astype(BF16)
    spec = lambda off: pl.BlockSpec((1, L, w), lambda b, j: (b, 0, off * nb + j))
    return pl.pallas_call(
        functools.partial(_hgrn_kernel, heads=heads),
        grid=(B, nb),
        in_specs=[spec(0), spec(1), spec(2), spec(3), spec(4),
                  pl.BlockSpec((1, w), lambda b, j: (0, j)),
                  pl.BlockSpec((1, HG_DV), lambda b, j: (0, 0)),
                  pl.BlockSpec((2, HG_BLOCK, HG_BLOCK), lambda b, j: (0, 0, 0))],
        out_specs=pl.BlockSpec((1, L, w), lambda b, j: (b, 0, j)),
        out_shape=jax.ShapeDtypeStruct((B, L, HG_HEADS * HG_DV), BF16),
        scratch_shapes=[pltpu.VMEM((L, w), F32), pltpu.VMEM((L, w), F32),
                        pltpu.VMEM((2 * heads, HG_DV, HG_DK), F32)],
        compiler_params=_params(("parallel", "arbitrary")),
        name="hgrn2",
    )(p, p, p, p, p, lb.reshape(1, -1).astype(F32), out_norm.reshape(1, HG_DV).astype(F32), m)


def _mla_q_kernel(c_ref, gn_ref, w_ref, gh_ref, cos_ref, sin_ref, o_ref, xn_ref):
    @pl.when(pl.program_id(1) == 0)
    def _():
        xn_ref[...] = (_rms_scale(c_ref[...]) * gn_ref[...]).astype(BF16)

    hw = 2 * LANES
    head_dot = lambda g: _dot(xn_ref[...], w_ref[:, g * hw:(g + 1) * hw])
    y_next = head_dot(0)
    for g in range(MLA_HEADS_PER_STEP):
        y, y_next = y_next, (head_dot(g + 1) if g + 1 < MLA_HEADS_PER_STEP else None)
        inv = _inv_rms_lanes(y, MLA_NOPE + MLA_ROPE)
        gh = gh_ref[...]
        o_ref[:, g * hw:g * hw + MLA_NOPE] = (y[:, :MLA_NOPE] * inv * gh[:, :MLA_NOPE]).astype(o_ref.dtype)
        o_ref[:, g * hw + MLA_NOPE:(g + 1) * hw] = _rope_pairs(
            y[:, MLA_NOPE:] * inv * gh[:, MLA_NOPE:], cos_ref[...], sin_ref[...]).astype(o_ref.dtype)


def _mla_kv_kernel(c_ref, kr_ref, gn_ref, w_ref, gh_ref, cos_ref, sin_ref, k_ref, v_ref, xn_ref):
    @pl.when(pl.program_id(1) == 0)
    def _():
        xn_ref[...] = (_rms_scale(c_ref[...]) * gn_ref[...]).astype(BF16)

    hw = 2 * LANES
    kr = kr_ref[...]
    gh = gh_ref[...]
    head_dot = lambda g: _dot(xn_ref[...], w_ref[:, g * hw:(g + 1) * hw])
    y_next = head_dot(0)
    for g in range(MLA_HEADS_PER_STEP):
        y, y_next = y_next, (head_dot(g + 1) if g + 1 < MLA_HEADS_PER_STEP else None)
        kn = y[:, :MLA_NOPE]
        inv = _inv_rms_lanes(jnp.concatenate([kn, kr], axis=1), MLA_NOPE + MLA_ROPE)
        k_ref[:, g * hw:g * hw + MLA_NOPE] = (kn * inv * gh[:, :MLA_NOPE]).astype(k_ref.dtype)
        k_ref[:, g * hw + MLA_NOPE:(g + 1) * hw] = _rope_pairs(
            kr * inv * gh[:, MLA_NOPE:], cos_ref[...], sin_ref[...]).astype(k_ref.dtype)
        v_ref[:, g * MLA_V:(g + 1) * MLA_V] = y[:, MLA_NOPE:].astype(v_ref.dtype)


def _mla_qkv(c, L, q_norm, kv_norm, wq, wkv, gq, gk, tabs, tm=1024):
    T = c.shape[0]
    H = MLA_HEADS
    hps = MLA_HEADS_PER_STEP
    tm = _tile(L, tm)
    lt = L // tm
    hw = 2 * LANES
    tab_spec = pl.BlockSpec((tm, LANES), lambda i, h: (i % lt, 0))
    row = lambda n: pl.BlockSpec((1, n), lambda i, h: (0, 0))
    q = pl.pallas_call(
        _mla_q_kernel,
        grid=(T // tm, H // hps),
        in_specs=[pl.BlockSpec((tm, MLA_Q_RANK), lambda i, h: (i, 0)), row(MLA_Q_RANK),
                  pl.BlockSpec((MLA_Q_RANK, hps * hw), lambda i, h: (0, h)), row(hw),
                  tab_spec, tab_spec],
        out_specs=pl.BlockSpec((tm, hps * hw), lambda i, h: (i, h)),
        out_shape=jax.ShapeDtypeStruct((T, H * hw), BF16),
        scratch_shapes=[pltpu.VMEM((tm, MLA_Q_RANK), BF16)],
        compiler_params=_params(("parallel", "arbitrary")),
        name="mla_q",
    )(c, q_norm.reshape(1, -1).astype(F32), wq, gq, *tabs)
    k, v = pl.pallas_call(
        _mla_kv_kernel,
        grid=(T // tm, H // hps),
        in_specs=[pl.BlockSpec((tm, MLA_KV_RANK), lambda i, h: (i, 1)),
                  pl.BlockSpec((tm, LANES), lambda i, h: (i, (MLA_Q_RANK + MLA_KV_RANK) // LANES)),
                  row(MLA_KV_RANK),
                  pl.BlockSpec((MLA_KV_RANK, hps * hw), lambda i, h: (0, h)), row(hw),
                  tab_spec, tab_spec],
        out_specs=[pl.BlockSpec((tm, hps * hw), lambda i, h: (i, h)),
                   pl.BlockSpec((tm, hps * MLA_V), lambda i, h: (i, h))],
        out_shape=[jax.ShapeDtypeStruct((T, H * hw), BF16), jax.ShapeDtypeStruct((T, H * MLA_V), BF16)],
        scratch_shapes=[pltpu.VMEM((tm, MLA_KV_RANK), BF16)],
        compiler_params=_params(("parallel", "arbitrary")),
        name="mla_kv",
    )(c, c, kv_norm.reshape(1, -1).astype(F32), wkv, gk, *tabs)
    return q, k, v


def _head_norm_rope_kernel(x_ref, g_ref, cos_ref, sin_ref, o_ref, *, n_heads):
    for h in range(n_heads):
        cs = slice(h * LANES, (h + 1) * LANES)
        x = x_ref[:, cs].astype(F32)
        y = x * _inv_rms_lanes(x, LANES) * g_ref[:, cs]
        o_ref[:, cs] = _rope_pairs(y, cos_ref[...], sin_ref[...]).astype(o_ref.dtype)


def _head_norm_rope(p, L, n_heads, gains, tabs, tm=512):
    T = p.shape[0]
    tm = _tile(L, tm)
    lt = L // tm
    w = n_heads * LANES
    tab_spec = pl.BlockSpec((tm, LANES), lambda i: (i % lt, 0))
    return pl.pallas_call(
        functools.partial(_head_norm_rope_kernel, n_heads=n_heads),
        grid=(T // tm,),
        in_specs=[pl.BlockSpec((tm, w), lambda i: (i, 0)),
                  pl.BlockSpec((1, w), lambda i: (0, 0)),
                  tab_spec, tab_spec],
        out_specs=pl.BlockSpec((tm, w), lambda i: (i, 0)),
        out_shape=jax.ShapeDtypeStruct((T, w), BF16),
        compiler_params=_params(("parallel",)),
        name="head_norm_rope",
    )(p, gains, *tabs)


ROPE_HALF = 32


def _rope_tables(pos_a, pos_b, base):
    freqs = base ** (-jnp.arange(ROPE_HALF, dtype=F32) / ROPE_HALF)

    def cs(pos):
        ang = pos[:, None] * freqs[None, :]
        return jnp.cos(ang), jnp.sin(ang)

    ca, sa = cs(pos_a)
    cb, sb = (jnp.zeros_like(ca), jnp.zeros_like(sa)) if pos_b is None else cs(pos_b)
    return jnp.concatenate([ca, cb, ca, cb], axis=-1), jnp.concatenate([-sa, -sb, sa, sb], axis=-1)


def _spread_pairs(a):
    z = jnp.zeros(a.shape[:-1] + (ROPE_HALF,), a.dtype)
    return jnp.concatenate([a[..., :ROPE_HALF], z, a[..., ROPE_HALF:], z], axis=-1)


def _interleave_halves(a):
    q = [a[..., i * ROPE_HALF:(i + 1) * ROPE_HALF] for i in range(4)]
    return jnp.concatenate([q[0], q[2], q[1], q[3]], axis=-1)


def _trunk(x, mem, w, group_batches):
    B, L, D = x.shape
    T = B * L
    M = mem.shape[1]
    x = x.reshape(T, D)
    mem = mem.reshape(B * M, D)
    pos = jnp.arange(L, dtype=F32)
    sm = jax.nn.softmax(w['hg_lb'].astype(F32), axis=0)
    lb_all = jnp.cumsum(sm, axis=0) - sm[0]
    depth = w['norm_mix'].shape[0]
    for i in range(depth):
        kind, j = i % 4, i // 4
        if kind == 0:
            p = _norm_matmul(x, w['norm_mix'][i], w['ret_w_in'][j], BF16)
            half = RET_DK // 2
            ang = pos[:, None] * (RET_ROPE_BASE ** (-jnp.arange(half, dtype=F32) / half))[None, :]
            dec = jnp.broadcast_to(w['ret_decay'][j].astype(F32).T[:, :, None], (RET_HEADS, 2, LANES))
            o = _retention(p.reshape(B, L, -1), jnp.cos(ang), jnp.sin(ang), dec, w['ret_out_norm'][j])
            x = _matmul_residual(o.reshape(T, -1), w['ret_w_out'][j], x)
        elif kind == 1:
            p = _norm_matmul(x, w['norm_mix'][i], w['hg_w_in'][j], BF16)
            o = _hgrn(p.reshape(B, L, -1), lb_all[i], w['hg_out_norm'][j])
            x = _matmul_residual(o.reshape(T, -1), w['hg_w_out'][j], x, tm=512, tn=D)
        elif kind == 2:
            c = _norm_matmul(x, w['norm_mix'][i], w['mla_w_in'][j], F32)
            tabs = _rope_tables(pos, None, MLA_ROPE_BASE)
            q, k, v = _mla_qkv(c, L, w['mla_q_norm'][j], w['mla_kv_norm'][j], w['mla_w_qb'][j], w['mla_w_kvb'][j],
                               w['mla_gq'][j], w['mla_gk'][j], tabs)
            o = _attention(q.reshape(B, L, -1), k.reshape(B, L, -1), v.reshape(B, L, -1),
                           steps=MLA_HEADS // MLA_HEADS_PER_STEP, heads=MLA_HEADS_PER_STEP, shared_kv=False,
                           dq=2 * LANES, dv=MLA_V, k_block0=0, v_block0=0)
            x = _matmul_residual(o.reshape(T, -1), w['mla_w_out'][j], x, tm=512, tn=D)
        else:
            p = _norm_matmul(x, w['norm_mix'][i], w['gqa_w_in'][j], BF16)
            t = jnp.arange(L)
            tabs = _rope_tables((t // GRID_W).astype(F32), (t % GRID_W).astype(F32), GQA_ROPE_BASE)
            nqk = GQA_HEADS + GQA_KV_HEADS
            qk = _head_norm_rope(p, L, nqk, w['gqa_gains'][j], tabs)
            o = _attention(qk.reshape(B, L, -1), qk.reshape(B, L, -1), p.reshape(B, L, -1),
                           steps=GQA_KV_HEADS, heads=GQA_HEADS // GQA_KV_HEADS, shared_kv=True,
                           dq=GQA_HD, dv=GQA_HD, k_block0=GQA_HEADS, v_block0=nqk)
            x = _matmul_residual(o.reshape(T, -1), w['gqa_w_out'][j], x, tm=512, tn=D)
        kv = _norm_matmul(mem, w['norm_memtok'][i], w['mem_w_kv'][i], BF16)
        x = _xattn(x.reshape(B, L, D), kv.reshape(B, M, -1), w['norm_mem'][i], w['mem_w_q'][i],
                   w['mem_qk_norm'][i, 0], w['mem_qk_norm'][i, 1], w['mem_w_out'][i]).reshape(T, D)
        mlp_w = (w['norm_mlp'][i], w['mlp_w1'][i], w['mlp_w2'][i])
        if i < depth - 1:
            x = _mlp(x, *mlp_w)
    outs = []
    row0 = 0
    for nb in group_batches:
        outs.append(_mlp(x, *mlp_w, row0=row0, rows=nb * L).reshape(nb, L, D))
        row0 += nb * L
    return tuple(outs)


def _prepare_weights(w):
    out = dict(w)
    for name in ('ret_w_in', 'ret_w_out', 'hg_w_in', 'hg_w_out', 'mla_w_out', 'gqa_w_in', 'gqa_w_out',
                 'mem_w_q', 'mem_w_kv', 'mem_w_out', 'mlp_w1', 'mlp_w2', 'mla_w_kvb'):
        out[name] = w[name].astype(BF16)
    n = w['mla_w_in'].shape[0]
    lat = MLA_Q_RANK + MLA_KV_RANK
    out['mla_w_in'] = jnp.concatenate([w['mla_w_in'][..., :lat], _spread_pairs(w['mla_w_in'][..., lat:])],
                                      axis=-1).astype(BF16)
    wq = w['mla_w_qb'].reshape(n, MLA_Q_RANK, MLA_HEADS, MLA_NOPE + MLA_ROPE)
    wq = jnp.concatenate([wq[..., :MLA_NOPE], _spread_pairs(wq[..., MLA_NOPE:])], axis=-1)
    out['mla_w_qb'] = wq.reshape(n, MLA_Q_RANK, -1).astype(BF16)
    g = w['mla_qk_norm'].astype(F32)
    g = jnp.concatenate([g[..., :MLA_NOPE], _spread_pairs(g[..., MLA_NOPE:])], axis=-1)
    out['mla_gq'] = g[:, 0:1] * ((MLA_NOPE + MLA_ROPE) ** -0.5 * LOG2_E)
    out['mla_gk'] = g[:, 1:2]
    nqk = GQA_HEADS + GQA_KV_HEADS
    wi = w['gqa_w_in']
    wqk = _interleave_halves(wi[..., :nqk * GQA_HD].reshape(wi.shape[0], wi.shape[1], nqk, GQA_HD))
    out['gqa_w_in'] = jnp.concatenate([wqk.reshape(wi.shape[0], wi.shape[1], -1), wi[..., nqk * GQA_HD:]],
                                      axis=-1).astype(BF16)
    g = _interleave_halves(w['gqa_qk_norm'].astype(F32))
    out['gqa_gains'] = jnp.concatenate([jnp.tile(g[:, 0] * (GQA_HD ** -0.5 * LOG2_E), (1, GQA_HEADS)),
                                        jnp.tile(g[:, 1], (1, GQA_KV_HEADS))], axis=-1)[:, None, :]
    return out


def kernel(x_prompt, x_sample, mem_prompt, mem_sample, norm_mix, norm_mem, norm_memtok, norm_mlp, ret_w_in, ret_decay, ret_out_norm, ret_w_out, hg_w_in, hg_lb, hg_out_norm, hg_w_out, mla_w_in, mla_q_norm, mla_kv_norm, mla_w_qb, mla_w_kvb, mla_qk_norm, mla_w_out, gqa_w_in, gqa_qk_norm, gqa_w_out, mem_w_q, mem_w_kv, mem_qk_norm, mem_w_out, mlp_w1, mlp_w2):
    w = _prepare_weights(dict(
        norm_mix=norm_mix, norm_mem=norm_mem, norm_memtok=norm_memtok, norm_mlp=norm_mlp,
        ret_w_in=ret_w_in, ret_decay=ret_decay, ret_out_norm=ret_out_norm, ret_w_out=ret_w_out,
        hg_w_in=hg_w_in, hg_lb=hg_lb, hg_out_norm=hg_out_norm, hg_w_out=hg_w_out,
        mla_w_in=mla_w_in, mla_q_norm=mla_q_norm, mla_kv_norm=mla_kv_norm, mla_w_qb=mla_w_qb,
        mla_w_kvb=mla_w_kvb, mla_qk_norm=mla_qk_norm, mla_w_out=mla_w_out,
        gqa_w_in=gqa_w_in, gqa_qk_norm=gqa_qk_norm, gqa_w_out=gqa_w_out,
        mem_w_q=mem_w_q, mem_w_kv=mem_w_kv, mem_qk_norm=mem_qk_norm, mem_w_out=mem_w_out,
        mlp_w1=mlp_w1, mlp_w2=mlp_w2))
    x = jnp.concatenate([x_prompt, x_sample], axis=0)
    mem = jnp.concatenate([mem_prompt, mem_sample], axis=0)
    return _trunk(x, mem, w, (x_prompt.shape[0], x_sample.shape[0]))
```

```python
import functools

import jax
import jax.numpy as jnp
from jax import lax
from jax.experimental import pallas as pl
from jax.experimental.pallas import tpu as pltpu

F32 = jnp.float32
BF16 = jnp.bfloat16

D_MODEL = 2048
GRID_W = 64
NORM_EPS = 1e-6
LOG2_E = 1.4426950408889634
RET_HEADS, RET_DK, RET_DV = 8, 256, 512
RET_ROPE_BASE = 10000.0
HG_HEADS, HG_DK, HG_DV, HG_CHUNK = 16, 128, 128, 32
MLA_HEADS, MLA_Q_RANK, MLA_KV_RANK, MLA_NOPE, MLA_ROPE, MLA_V = 16, 512, 512, 128, 64, 128
MLA_ROPE_BASE = 10000.0
GQA_HEADS, GQA_KV_HEADS, GQA_HD = 16, 4, 128
GQA_ROPE_BASE = 10000.0
MEM_HEADS, MEM_HD = 4, 128

LANES = 128
VMEM_LIMIT_BYTES = 56 * 2 ** 20

HG_BLOCK = 128
HG_HEADS_PER_STEP = 4
MLA_HEADS_PER_STEP = 4


def _params(sem):
    return pltpu.CompilerParams(dimension_semantics=sem, vmem_limit_bytes=VMEM_LIMIT_BYTES)


def _tile(n, pref):
    return pref if n % pref == 0 else n


def _rms_scale(x, width=None):
    width = x.shape[-1] if width is None else width
    ms = jnp.sum(x * x, axis=-1, keepdims=True) * (1.0 / width)
    return x * lax.rsqrt(ms + NORM_EPS)


def _inv_rms_lanes(x, width):
    sq = (x * x).astype(BF16)
    ss = _dot(sq, jnp.ones((x.shape[-1], LANES), BF16))
    return lax.rsqrt(ss * (1.0 / width) + NORM_EPS)


def _sigmoid(x):
    return 1.0 / (1.0 + jnp.exp(-x))


def _dot(a, b):
    return jnp.dot(a, b, preferred_element_type=F32)


def _dot_nt(a, b):
    return lax.dot_general(a, b, (((1,), (1,)), ((), ())), preferred_element_type=F32)


def _rope_pairs(x, c, s):
    return x * c + pltpu.roll(x, LANES // 2, 1) * s


def _row_parts(xs, tm):
    bounds, lo = [], 0
    for x in xs:
        assert x.shape[0] % tm == 0
        bounds.append((lo, x.shape[0] // tm))
        lo += x.shape[0] // tm

    def index_map(part, col):
        lo, n = bounds[part]
        return lambda i, j: (jnp.clip(i - lo, 0, n - 1), jnp.where((i >= lo) & (i < lo + n), col(j), 0))

    return bounds, index_map


def _select_part(bounds, fn):
    i = pl.program_id(0)
    for part, (lo, n) in enumerate(bounds):
        if len(bounds) == 1:
            fn(part)
        else:
            pl.when((i >= lo) & (i < lo + n))(functools.partial(fn, part))


def _norm_matmul_kernel(*refs, bounds):
    x_refs = refs[:len(bounds)]
    g_ref, w_ref, o_ref, xn_ref = refs[len(bounds):]

    def normalise(part):
        x = x_refs[part][...].astype(F32)
        xn_ref[...] = (_rms_scale(x) * g_ref[...]).astype(BF16)

    @pl.when(pl.program_id(1) == 0)
    def _():
        _select_part(bounds, normalise)

    o_ref[...] = _dot(xn_ref[...], w_ref[...]).astype(o_ref.dtype)


def _norm_matmul(xs, g, w, out_dtype, tm=1024, tn=1024):
    xs = xs if isinstance(xs, (list, tuple)) else [xs]
    T, K = sum(x.shape[0] for x in xs), xs[0].shape[1]
    N = w.shape[1]
    tm, tn = _tile(min(x.shape[0] for x in xs), tm), _tile(N, tn)
    bounds, index_map = _row_parts(xs, tm)
    return pl.pallas_call(
        functools.partial(_norm_matmul_kernel, bounds=bounds),
        grid=(T // tm, N // tn),
        in_specs=[pl.BlockSpec((tm, K), index_map(p, lambda j: 0)) for p in range(len(xs))] + [
                  pl.BlockSpec((1, K), lambda i, j: (0, 0)),
                  pl.BlockSpec((K, tn), lambda i, j: (0, j))],
        out_specs=pl.BlockSpec((tm, tn), lambda i, j: (i, j)),
        out_shape=jax.ShapeDtypeStruct((T, N), out_dtype),
        scratch_shapes=[pltpu.VMEM((tm, K), BF16)],
        compiler_params=_params(("parallel", "arbitrary")),
        name="norm_matmul",
    )(*xs, g.reshape(1, K).astype(F32), w)


def _matmul_residual_kernel(a_ref, w_ref, *refs, bounds):
    x_refs, o_ref = refs[:len(bounds)], refs[len(bounds)]
    acc = _dot(a_ref[...], w_ref[...])

    def add(part):
        o_ref[...] = x_refs[part][...] + acc

    _select_part(bounds, add)


def _matmul_residual(a, w, xs, tm=1024, tn=512):
    xs = xs if isinstance(xs, (list, tuple)) else [xs]
    T, K = a.shape
    N = w.shape[1]
    tm, tn = _tile(min(x.shape[0] for x in xs), tm), _tile(N, tn)
    bounds, index_map = _row_parts(xs, tm)
    return pl.pallas_call(
        functools.partial(_matmul_residual_kernel, bounds=bounds),
        grid=(T // tm, N // tn),
        in_specs=[pl.BlockSpec((tm, K), lambda i, j: (i, 0)),
                  pl.BlockSpec((K, tn), lambda i, j: (0, j))] + [
                  pl.BlockSpec((tm, tn), index_map(p, lambda j: j)) for p in range(len(xs))],
        out_specs=pl.BlockSpec((tm, tn), lambda i, j: (i, j)),
        out_shape=jax.ShapeDtypeStruct((T, N), F32),
        input_output_aliases={2: 0} if len(xs) == 1 else {},
        compiler_params=_params(("parallel", "arbitrary")),
        name="matmul_residual",
    )(a, w, *xs)


def _mlp_kernel(x_ref, g_ref, w1_ref, w2_ref, o_ref, xn_ref):
    @pl.when(pl.program_id(1) == 0)
    def _():
        x = x_ref[...]
        xn_ref[...] = (_rms_scale(x) * g_ref[...]).astype(BF16)
        o_ref[...] = x

    a = jnp.maximum(_dot(xn_ref[...], w1_ref[...]), 0.0)
    o_ref[...] += _dot((a * a).astype(BF16), w2_ref[...])


def _mlp(x, g, w1, w2, layer, row0=0, rows=None, tm=512, tf=1024):
    D = x.shape[1]
    rows = x.shape[0] if rows is None else rows
    Fd = w1.shape[2]
    tm, tf = _tile(rows, tm), _tile(Fd, tf)
    assert row0 % tm == 0
    blk0 = row0 // tm
    return pl.pallas_call(
        _mlp_kernel,
        grid=(rows // tm, Fd // tf),
        in_specs=[pl.BlockSpec((tm, D), lambda i, f: (blk0 + i, 0)),
                  pl.BlockSpec((1, D), lambda i, f: (0, 0)),
                  pl.BlockSpec((None, D, tf), lambda i, f: (layer, 0, f)),
                  pl.BlockSpec((None, tf, D), lambda i, f: (layer, f, 0))],
        out_specs=pl.BlockSpec((tm, D), lambda i, f: (i, 0)),
        out_shape=jax.ShapeDtypeStruct((rows, D), F32),
        scratch_shapes=[pltpu.VMEM((tm, D), BF16)],
        compiler_params=_params(("parallel", "arbitrary")),
        name="mlp",
    )(x, g.reshape(1, D).astype(F32), w1, w2)


def _xattn_kernel(x_ref, g_ref, wq_ref, kv_ref, gq_ref, gk_ref, wo_ref, o_ref):
    x = x_ref[0]
    xn = (_rms_scale(x) * g_ref[...]).astype(BF16)
    q = _dot(xn, wq_ref[...])
    kv = kv_ref[0].astype(F32)
    width = MEM_HEADS * MEM_HD
    outs = []
    for h in range(MEM_HEADS):
        cs = slice(h * MEM_HD, (h + 1) * MEM_HD)
        qh = (_rms_scale(q[:, cs]) * gq_ref[...]).astype(BF16)
        kh = (_rms_scale(kv[:, cs]) * gk_ref[...]).astype(BF16)
        vh = kv_ref[0, :, width + h * MEM_HD: width + (h + 1) * MEM_HD]
        s = _dot_nt(qh, kh)
        p = jnp.exp(s - jnp.max(s, axis=-1, keepdims=True))
        l = jnp.sum(p, axis=-1, keepdims=True)
        outs.append((_dot(p.astype(BF16), vh) / l).astype(BF16))
    o = jnp.concatenate(outs, axis=-1)
    o_ref[0] = x + _dot(o, wo_ref[...])


def _xattn(x, kv, g, wq, gq, gk, wo, tm=512):
    B, L, D = x.shape
    M = kv.shape[1]
    width = MEM_HEADS * MEM_HD
    tm = _tile(L, tm)
    return pl.pallas_call(
        _xattn_kernel,
        grid=(B, L // tm),
        in_specs=[pl.BlockSpec((1, tm, D), lambda b, i: (b, i, 0)),
                  pl.BlockSpec((1, D), lambda b, i: (0, 0)),
                  pl.BlockSpec((D, width), lambda b, i: (0, 0)),
                  pl.BlockSpec((1, M, 2 * width), lambda b, i: (b, 0, 0)),
                  pl.BlockSpec((1, MEM_HD), lambda b, i: (0, 0)),
                  pl.BlockSpec((1, MEM_HD), lambda b, i: (0, 0)),
                  pl.BlockSpec((width, D), lambda b, i: (0, 0))],
        out_specs=pl.BlockSpec((1, tm, D), lambda b, i: (b, i, 0)),
        out_shape=jax.ShapeDtypeStruct((B, L, D), F32),
        input_output_aliases={0: 0},
        compiler_params=_params(("parallel", "arbitrary")),
        name="mem_xattn",
    )(x, g.reshape(1, D).astype(F32), wq, kv,
      (gq.astype(F32) * (MEM_HD ** -0.5)).reshape(1, MEM_HD), gk.reshape(1, MEM_HD).astype(F32), wo)


def _attn_kernel(q_ref, k_ref, v_ref, o_ref, *, heads, shared_kv, dq, dv):
    def scores(g):
        kg = 0 if shared_kv else g
        return _dot_nt(q_ref[0, :, g * dq:(g + 1) * dq], k_ref[0, :, kg * dq:(kg + 1) * dq])

    s_next = scores(0)
    for g in range(heads):
        s, s_next = s_next, (scores(g + 1) if g + 1 < heads else None)
        kg = 0 if shared_kv else g
        v = v_ref[0, :, kg * dv:(kg + 1) * dv]
        p = jnp.exp2(s - jnp.max(s, axis=-1, keepdims=True))
        l = jnp.sum(p, axis=-1, keepdims=True)
        o_ref[0, :, g * dv:(g + 1) * dv] = (_dot(p.astype(BF16), v) / l).astype(o_ref.dtype)


def _attention(q, k, v, *, steps, heads, shared_kv, dq, dv, k_block0, v_block0, tq=512):
    B, L = q.shape[0], q.shape[1]
    tq = _tile(L, tq)
    kvh = 1 if shared_kv else heads
    return pl.pallas_call(
        functools.partial(_attn_kernel, heads=heads, shared_kv=shared_kv, dq=dq, dv=dv),
        grid=(B, steps, L // tq),
        in_specs=[pl.BlockSpec((1, tq, heads * dq), lambda b, h, i: (b, i, h)),
                  pl.BlockSpec((1, L, kvh * dq), lambda b, h, i: (b, 0, k_block0 + h)),
                  pl.BlockSpec((1, L, kvh * dv), lambda b, h, i: (b, 0, v_block0 + h))],
        out_specs=pl.BlockSpec((1, tq, heads * dv), lambda b, h, i: (b, i, h)),
        out_shape=jax.ShapeDtypeStruct((B, L, steps * heads * dv), BF16),
        compiler_params=_params(("parallel", "parallel", "arbitrary")),
        name="softmax_attention",
    )(q, k, v)


def _retention_kernel(q_ref, k_ref, v_ref, g_ref, cos_ref, sin_ref, dec_ref, on_ref, o_ref, *, tq):
    L = q_ref.shape[1]
    half = RET_DK // 2
    cos = cos_ref[...]
    sin = sin_ref[...]

    def rope(ref, scale):
        x1 = ref[0, :, :half].astype(F32)
        x2 = ref[0, :, half:].astype(F32)
        y = jnp.concatenate([x1 * cos - x2 * sin, x1 * sin + x2 * cos], axis=-1)
        return (y * scale).astype(BF16)

    q = rope(q_ref, RET_DK ** -0.5)
    k = rope(k_ref, 1.0)
    v = v_ref[0]
    dec = dec_ref[0]
    lg = jnp.minimum(dec, 0.0) - jnp.log1p(jnp.exp(-jnp.abs(dec)))
    lg_f = lg[0:1, 0:1]
    lg_b = lg[1:2, 0:1]
    n = L // tq
    ri = lax.broadcasted_iota(jnp.int32, (tq, tq), 0)
    diff = (ri - lax.broadcasted_iota(jnp.int32, (tq, tq), 1)).astype(F32)
    d_intra = jnp.where(diff == 0, 2.0, jnp.exp(jnp.where(diff >= 0, diff * lg_f, -diff * lg_b)))
    r = lax.broadcasted_iota(jnp.int32, (tq, RET_DK), 0).astype(F32)
    qdec = (jnp.exp((r + 1.0) * lg_f), jnp.exp((tq - r) * lg_b))
    kdec = (jnp.exp((tq - 1.0 - r) * lg_f), jnp.exp(r * lg_b))
    gc = (jnp.exp(tq * lg_f), jnp.exp(tq * lg_b))
    rows = [slice(i * tq, (i + 1) * tq) for i in range(n)]

    def kv_state(m, d):
        kd = (k[rows[m]].astype(F32) * kdec[d]).T.astype(BF16)
        return _dot(kd, v[rows[m]])

    state = [[None] * n, [None] * n]
    for i in range(1, n):
        prev = state[0][i - 1]
        state[0][i] = kv_state(i - 1, 0) if prev is None else prev * gc[0] + kv_state(i - 1, 0)
    for i in range(n - 2, -1, -1):
        nxt = state[1][i + 1]
        state[1][i] = kv_state(i + 1, 1) if nxt is None else nxt * gc[1] + kv_state(i + 1, 1)
    for i in range(n):
        qi = q[rows[i]]
        s = _dot_nt(qi, k[rows[i]]) * d_intra
        o = _dot(s.astype(BF16), v[rows[i]])
        for d in (0, 1):
            if state[d][i] is not None:
                o = o + _dot((qi.astype(F32) * qdec[d]).astype(BF16), state[d][i].astype(BF16))
        y = _rms_scale(o) * on_ref[...]
        gate = g_ref[0, rows[i], :].astype(F32)
        o_ref[0, rows[i], :] = (y * (gate * _sigmoid(gate))).astype(o_ref.dtype)


def _retention(p, cos, sin, dec, out_norm, tq=512):
    B, L, _ = p.shape
    H = RET_HEADS
    tq = _tile(L, tq)
    v_block0 = 2 * H * RET_DK // RET_DV
    return pl.pallas_call(
        functools.partial(_retention_kernel, tq=tq),
        grid=(H, B),
        in_specs=[pl.BlockSpec((1, L, RET_DK), lambda h, b: (b, 0, h)),
                  pl.BlockSpec((1, L, RET_DK), lambda h, b: (b, 0, H + h)),
                  pl.BlockSpec((1, L, RET_DV), lambda h, b: (b, 0, v_block0 + h)),
                  pl.BlockSpec((1, L, RET_DV), lambda h, b: (b, 0, v_block0 + H + h)),
                  pl.BlockSpec((L, RET_DK // 2), lambda h, b: (0, 0)),
                  pl.BlockSpec((L, RET_DK // 2), lambda h, b: (0, 0)),
                  pl.BlockSpec((1, 2, LANES), lambda h, b: (h, 0, 0)),
                  pl.BlockSpec((1, RET_DV), lambda h, b: (0, 0))],
        out_specs=pl.BlockSpec((1, L, RET_DV), lambda h, b: (b, 0, h)),
        out_shape=jax.ShapeDtypeStruct((B, L, H * RET_DV), BF16),
        compiler_params=_params(("parallel", "arbitrary")),
        name="retention",
    )(p, p, p, p, cos, sin, dec, out_norm.reshape(1, RET_DV).astype(F32))


def _hgrn_kernel(q_ref, ff_ref, fb_ref, i_ref, g_ref, lb_ref, on_ref, m_ref, o_ref,
                 of_ref, ob_ref, st_ref, qt_ref, kt_ref, ks_ref, dec_ref, *, heads):
    L = q_ref.shape[1]
    nblk = L // HG_BLOCK
    n_chunks = HG_BLOCK // HG_CHUNK
    width = heads * HG_DK
    rows = lax.broadcasted_iota(jnp.int32, (HG_BLOCK, HG_BLOCK), 0)
    cols = lax.broadcasted_iota(jnp.int32, (HG_BLOCK, HG_BLOCK), 1)
    same_chunk = (rows // HG_CHUNK) == (cols // HG_CHUNK)
    masks = (same_chunk & (cols <= rows), same_chunk & (cols >= rows))
    row_chunk = lax.broadcasted_iota(jnp.int32, (HG_BLOCK, HG_DK), 0) // HG_CHUNK
    st_ref[...] = jnp.zeros_like(st_ref)

    def block_rows(t, d):
        r = t if d == 0 else nblk - 1 - t
        return pl.ds(pl.multiple_of(r * HG_BLOCK, HG_BLOCK), HG_BLOCK)

    def prep(t, d, slot):
        rs = block_rows(t, d)
        fz = (ff_ref if d == 0 else fb_ref)[0, rs, :].astype(F32)
        lb = lb_ref[...]
        f = lb + (1.0 - lb) * _sigmoid(fz)
        lf = jnp.log(f)
        kk = 1.0 - f
        hi = lf.astype(BF16)
        lo = (lf - hi.astype(F32)).astype(BF16)
        cb = _dot(m_ref[d], jnp.concatenate([hi, lo], axis=1))
        b = cb[:, :width] + cb[:, width:]
        last = [c * HG_CHUNK + (HG_CHUNK - 1 if d == 0 else 0) for c in range(n_chunks)]
        dec_rows = [jnp.exp(b[i:i + 1, :]) for i in last]
        dec = jnp.concatenate([jnp.broadcast_to(row, (HG_CHUNK, width)) for row in dec_rows], axis=0)
        kt = kk * jnp.exp(-b)
        qt_ref[slot, d] = (q_ref[0, rs, :].astype(F32) * jnp.exp(b)).astype(BF16)
        kt_ref[slot, d] = kt.astype(BF16)
        ks_ref[slot, d] = (kt * dec).astype(BF16)
        for c in range(n_chunks):
            dec_ref[slot, d, c:c + 1, :] = dec_rows[c]

    items = [(h, d) for h in range(heads) for d in (0, 1)]

    def half(t, slot):
        rs = [block_rows(t, d) for d in (0, 1)]
        v = {(h, d): i_ref[0, rs[d], h * HG_DK:(h + 1) * HG_DK] for h, d in items}
        qt = {(h, d): qt_ref[slot, d, :, h * HG_DK:(h + 1) * HG_DK] for h, d in items}
        s = {(h, d): _dot_nt(qt[h, d], kt_ref[slot, d, :, h * HG_DK:(h + 1) * HG_DK]) for h, d in items}
        kv = {}
        for h, d in items:
            ks = ks_ref[slot, d, :, h * HG_DK:(h + 1) * HG_DK].astype(F32)
            ks_cols = jnp.concatenate([jnp.where(row_chunk == c, ks, 0.0) for c in range(n_chunks)], axis=1)
            kv[h, d] = _dot(v[h, d].astype(F32).T.astype(BF16), ks_cols.astype(BF16))
        t_next = jnp.minimum(t + 1, nblk - 1)
        prep(t_next, 0, 1 - slot)
        s = {it: jnp.where(masks[it[1]], s[it], 0.0).astype(BF16) for it in items}
        o = {it: _dot(s[it], v[it]) for it in items}
        st = {(h, d): st_ref[2 * h + d] for h, d in items}
        parts = {it: [None] * n_chunks for it in items}
        for step in range(n_chunks):
            if step == n_chunks // 2:
                prep(t_next, 1, 1 - slot)
            for h, d in items:
                c = step if d == 0 else n_chunks - 1 - step
                cr = slice(c * HG_CHUNK, (c + 1) * HG_CHUNK)
                parts[h, d][c] = o[h, d][cr] + _dot_nt(qt[h, d][cr], st[h, d].astype(BF16))
                dec = dec_ref[slot, d, c:c + 1, h * HG_DK:(h + 1) * HG_DK]
                st[h, d] = st[h, d] * dec + kv[h, d][:, c * HG_DK:(c + 1) * HG_DK]
        for h, d in items:
            st_ref[2 * h + d] = st[h, d]
        for d in (0, 1):
            out = jnp.concatenate([jnp.concatenate(parts[h, d], axis=0) for h in range(heads)], axis=1)
            (of_ref if d == 0 else ob_ref)[rs[d], :] = out

    def body(u, carry):
        half(2 * u, 0)
        half(2 * u + 1, 1)
        return carry

    for d in (0, 1):
        prep(jnp.int32(0), d, 0)
    lax.fori_loop(0, nblk // 2, body, 0)
    for h in range(heads):
        cs = slice(h * HG_DV, (h + 1) * HG_DV)
        y = _rms_scale(of_ref[:, cs] + ob_ref[:, cs]) * on_ref[...]
        gate = g_ref[0, :, cs].astype(F32)
        o_ref[0, :, cs] = (y * (gate * _sigmoid(gate))).astype(o_ref.dtype)


def _hgrn(p, lb, out_norm):
    B, L, _ = p.shape
    heads = HG_HEADS_PER_STEP
    nb = HG_HEADS // heads
    w = heads * HG_DK
    c = HG_CHUNK
    n_chunks = HG_BLOCK // HG_CHUNK
    assert (L // HG_BLOCK) % 2 == 0
    idx = jnp.arange(HG_BLOCK)
    same = (idx[:, None] // c) == (idx[None, :] // c)
    tri_f = same & (idx[None, :] <= idx[:, None])
    tri_b = same & (idx[None, :] >= idx[:, None])
    m = jnp.stack([tri_f, tri_b]).astype(BF16)
    spec = lambda off: pl.BlockSpec((1, L, w), lambda b, j: (b, 0, off * nb + j))
    return pl.pallas_call(
        functools.partial(_hgrn_kernel, heads=heads),
        grid=(B, nb),
        in_specs=[spec(0), spec(1), spec(2), spec(3), spec(4),
                  pl.BlockSpec((1, w), lambda b, j: (0, j)),
                  pl.BlockSpec((1, HG_DV), lambda b, j: (0, 0)),
                  pl.BlockSpec((2, HG_BLOCK, HG_BLOCK), lambda b, j: (0, 0, 0))],
        out_specs=pl.BlockSpec((1, L, w), lambda b, j: (b, 0, j)),
        out_shape=jax.ShapeDtypeStruct((B, L, HG_HEADS * HG_DV), BF16),
        scratch_shapes=[pltpu.VMEM((L, w), F32), pltpu.VMEM((L, w), F32),
                        pltpu.VMEM((2 * heads, HG_DV, HG_DK), F32),
                        pltpu.VMEM((2, 2, HG_BLOCK, w), BF16), pltpu.VMEM((2, 2, HG_BLOCK, w), BF16),
                        pltpu.VMEM((2, 2, HG_BLOCK, w), BF16), pltpu.VMEM((2, 2, n_chunks, w), F32)],
        compiler_params=_params(("parallel", "arbitrary")),
        name="hgrn2",
    )(p, p, p, p, p, lb.reshape(1, -1).astype(F32), out_norm.reshape(1, HG_DV).astype(F32), m)


def _mla_q_kernel(c_ref, gn_ref, w_ref, gh_ref, cos_ref, sin_ref, o_ref, xn_ref):
    @pl.when(pl.program_id(1) == 0)
    def _():
        xn_ref[...] = (_rms_scale(c_ref[...]) * gn_ref[...]).astype(BF16)

    hw = 2 * LANES
    head_dot = lambda g: _dot(xn_ref[...], w_ref[:, g * hw:(g + 1) * hw])
    y_next = head_dot(0)
    for g in range(MLA_HEADS_PER_STEP):
        y, y_next = y_next, (head_dot(g + 1) if g + 1 < MLA_HEADS_PER_STEP else None)
        inv = _inv_rms_lanes(y, MLA_NOPE + MLA_ROPE)
        gh = gh_ref[...]
        o_ref[:, g * hw:g * hw + MLA_NOPE] = (y[:, :MLA_NOPE] * inv * gh[:, :MLA_NOPE]).astype(o_ref.dtype)
        o_ref[:, g * hw + MLA_NOPE:(g + 1) * hw] = _rope_pairs(
            y[:, MLA_NOPE:] * inv * gh[:, MLA_NOPE:], cos_ref[...], sin_ref[...]).astype(o_ref.dtype)


def _mla_kv_kernel(c_ref, kr_ref, gn_ref, w_ref, gh_ref, cos_ref, sin_ref, k_ref, v_ref, xn_ref):
    @pl.when(pl.program_id(1) == 0)
    def _():
        xn_ref[...] = (_rms_scale(c_ref[...]) * gn_ref[...]).astype(BF16)

    hw = 2 * LANES
    kr = kr_ref[...]
    gh = gh_ref[...]
    head_dot = lambda g: _dot(xn_ref[...], w_ref[:, g * hw:(g + 1) * hw])
    y_next = head_dot(0)
    for g in range(MLA_HEADS_PER_STEP):
        y, y_next = y_next, (head_dot(g + 1) if g + 1 < MLA_HEADS_PER_STEP else None)
        kn = y[:, :MLA_NOPE]
        inv = _inv_rms_lanes(jnp.concatenate([kn, kr], axis=1), MLA_NOPE + MLA_ROPE)
        k_ref[:, g * hw:g * hw + MLA_NOPE] = (kn * inv * gh[:, :MLA_NOPE]).astype(k_ref.dtype)
        k_ref[:, g * hw + MLA_NOPE:(g + 1) * hw] = _rope_pairs(
            kr * inv * gh[:, MLA_NOPE:], cos_ref[...], sin_ref[...]).astype(k_ref.dtype)
        v_ref[:, g * MLA_V:(g + 1) * MLA_V] = y[:, MLA_NOPE:].astype(v_ref.dtype)


def _mla_qkv(c, L, q_norm, kv_norm, wq, wkv, gq, gk, tabs, tm=1024):
    T = c.shape[0]
    H = MLA_HEADS
    hps = MLA_HEADS_PER_STEP
    tm = _tile(L, tm)
    lt = L // tm
    hw = 2 * LANES
    tab_spec = pl.BlockSpec((tm, LANES), lambda i, h: (i % lt, 0))
    row = lambda n: pl.BlockSpec((1, n), lambda i, h: (0, 0))
    q = pl.pallas_call(
        _mla_q_kernel,
        grid=(T // tm, H // hps),
        in_specs=[pl.BlockSpec((tm, MLA_Q_RANK), lambda i, h: (i, 0)), row(MLA_Q_RANK),
                  pl.BlockSpec((MLA_Q_RANK, hps * hw), lambda i, h: (0, h)), row(hw),
                  tab_spec, tab_spec],
        out_specs=pl.BlockSpec((tm, hps * hw), lambda i, h: (i, h)),
        out_shape=jax.ShapeDtypeStruct((T, H * hw), BF16),
        scratch_shapes=[pltpu.VMEM((tm, MLA_Q_RANK), BF16)],
        compiler_params=_params(("parallel", "arbitrary")),
        name="mla_q",
    )(c, q_norm.reshape(1, -1).astype(F32), wq, gq, *tabs)
    k, v = pl.pallas_call(
        _mla_kv_kernel,
        grid=(T // tm, H // hps),
        in_specs=[pl.BlockSpec((tm, MLA_KV_RANK), lambda i, h: (i, 1)),
                  pl.BlockSpec((tm, LANES), lambda i, h: (i, (MLA_Q_RANK + MLA_KV_RANK) // LANES)),
                  row(MLA_KV_RANK),
                  pl.BlockSpec((MLA_KV_RANK, hps * hw), lambda i, h: (0, h)), row(hw),
                  tab_spec, tab_spec],
        out_specs=[pl.BlockSpec((tm, hps * hw), lambda i, h: (i, h)),
                   pl.BlockSpec((tm, hps * MLA_V), lambda i, h: (i, h))],
        out_shape=[jax.ShapeDtypeStruct((T, H * hw), BF16), jax.ShapeDtypeStruct((T, H * MLA_V), BF16)],
        scratch_shapes=[pltpu.VMEM((tm, MLA_KV_RANK), BF16)],
        compiler_params=_params(("parallel", "arbitrary")),
        name="mla_kv",
    )(c, c, kv_norm.reshape(1, -1).astype(F32), wkv, gk, *tabs)
    return q, k, v


def _head_norm_rope_kernel(x_ref, g_ref, cos_ref, sin_ref, o_ref, *, n_heads):
    for h in range(n_heads):
        cs = slice(h * LANES, (h + 1) * LANES)
        x = x_ref[:, cs].astype(F32)
        y = x * _inv_rms_lanes(x, LANES) * g_ref[:, cs]
        o_ref[:, cs] = _rope_pairs(y, cos_ref[...], sin_ref[...]).astype(o_ref.dtype)


def _head_norm_rope(p, L, n_heads, gains, tabs, tm=512):
    T = p.shape[0]
    tm = _tile(L, tm)
    lt = L // tm
    w = n_heads * LANES
    tab_spec = pl.BlockSpec((tm, LANES), lambda i: (i % lt, 0))
    return pl.pallas_call(
        functools.partial(_head_norm_rope_kernel, n_heads=n_heads),
        grid=(T // tm,),
        in_specs=[pl.BlockSpec((tm, w), lambda i: (i, 0)),
                  pl.BlockSpec((1, w), lambda i: (0, 0)),
                  tab_spec, tab_spec],
        out_specs=pl.BlockSpec((tm, w), lambda i: (i, 0)),
        out_shape=jax.ShapeDtypeStruct((T, w), BF16),
        compiler_params=_params(("parallel",)),
        name="head_norm_rope",
    )(p, gains, *tabs)


ROPE_HALF = 32


def _rope_tables(pos_a, pos_b, base):
    freqs = base ** (-jnp.arange(ROPE_HALF, dtype=F32) / ROPE_HALF)

    def cs(pos):
        ang = pos[:, None] * freqs[None, :]
        return jnp.cos(ang), jnp.sin(ang)

    ca, sa = cs(pos_a)
    cb, sb = (jnp.zeros_like(ca), jnp.zeros_like(sa)) if pos_b is None else cs(pos_b)
    return jnp.concatenate([ca, cb, ca, cb], axis=-1), jnp.concatenate([-sa, -sb, sa, sb], axis=-1)


def _spread_pairs(a):
    z = jnp.zeros(a.shape[:-1] + (ROPE_HALF,), a.dtype)
    return jnp.concatenate([a[..., :ROPE_HALF], z, a[..., ROPE_HALF:], z], axis=-1)


def _interleave_halves(a):
    q = [a[..., i * ROPE_HALF:(i + 1) * ROPE_HALF] for i in range(4)]
    return jnp.concatenate([q[0], q[2], q[1], q[3]], axis=-1)


def _trunk(xs, mem, w):
    group_batches = [x.shape[0] for x in xs]
    B, (L, D) = sum(group_batches), xs[0].shape[1:]
    T = B * L
    M = mem.shape[1]
    x = [x.reshape(-1, D) for x in xs]
    mem = mem.reshape(B * M, D)
    pos = jnp.arange(L, dtype=F32)
    sm = jax.nn.softmax(w['hg_lb'].astype(F32), axis=0)
    lb_all = jnp.cumsum(sm, axis=0) - sm[0]
    depth = w['norm_mix'].shape[0]
    for i in range(depth):
        kind, j = i % 4, i // 4
        if kind == 0:
            p = _norm_matmul(x, w['norm_mix'][i], w['ret_w_in'][j], BF16)
            half = RET_DK // 2
            ang = pos[:, None] * (RET_ROPE_BASE ** (-jnp.arange(half, dtype=F32) / half))[None, :]
            dec = jnp.broadcast_to(w['ret_decay'][j].astype(F32).T[:, :, None], (RET_HEADS, 2, LANES))
            o = _retention(p.reshape(B, L, -1), jnp.cos(ang), jnp.sin(ang), dec, w['ret_out_norm'][j])
            x = _matmul_residual(o.reshape(T, -1), w['ret_w_out'][j], x)
        elif kind == 1:
            p = _norm_matmul(x, w['norm_mix'][i], w['hg_w_in'][j], BF16)
            o = _hgrn(p.reshape(B, L, -1), lb_all[i], w['hg_out_norm'][j])
            x = _matmul_residual(o.reshape(T, -1), w['hg_w_out'][j], x, tm=512, tn=D)
        elif kind == 2:
            c = _norm_matmul(x, w['norm_mix'][i], w['mla_w_in'][j], F32)
            tabs = _rope_tables(pos, None, MLA_ROPE_BASE)
            q, k, v = _mla_qkv(c, L, w['mla_q_norm'][j], w['mla_kv_norm'][j], w['mla_w_qb'][j], w['mla_w_kvb'][j],
                               w['mla_gq'][j], w['mla_gk'][j], tabs)
            o = _attention(q.reshape(B, L, -1), k.reshape(B, L, -1), v.reshape(B, L, -1),
                           steps=MLA_HEADS // MLA_HEADS_PER_STEP, heads=MLA_HEADS_PER_STEP, shared_kv=False,
                           dq=2 * LANES, dv=MLA_V, k_block0=0, v_block0=0)
            x = _matmul_residual(o.reshape(T, -1), w['mla_w_out'][j], x, tm=512, tn=D)
        else:
            p = _norm_matmul(x, w['norm_mix'][i], w['gqa_w_in'][j], BF16)
            t = jnp.arange(L)
            tabs = _rope_tables((t // GRID_W).astype(F32), (t % GRID_W).astype(F32), GQA_ROPE_BASE)
            nqk = GQA_HEADS + GQA_KV_HEADS
            qk = _head_norm_rope(p, L, nqk, w['gqa_gains'][j], tabs)
            o = _attention(qk.reshape(B, L, -1), qk.reshape(B, L, -1), p.reshape(B, L, -1),
                           steps=GQA_KV_HEADS, heads=GQA_HEADS // GQA_KV_HEADS, shared_kv=True,
                           dq=GQA_HD, dv=GQA_HD, k_block0=GQA_HEADS, v_block0=nqk)
            x = _matmul_residual(o.reshape(T, -1), w['gqa_w_out'][j], x, tm=512, tn=D)
        kv = _norm_matmul(mem, w['norm_memtok'][i], w['mem_w_kv'][i], BF16)
        x = _xattn(x.reshape(B, L, D), kv.reshape(B, M, -1), w['norm_mem'][i], w['mem_w_q'][i],
                   w['mem_qk_norm'][i, 0], w['mem_qk_norm'][i, 1], w['mem_w_out'][i]).reshape(T, D)
        mlp_w = (w['norm_mlp'][i], w['mlp_w1'], w['mlp_w2'], i)
        if i < depth - 1:
            x = _mlp(x, *mlp_w)
    outs = []
    row0 = 0
    for nb in group_batches:
        outs.append(_mlp(x, *mlp_w, row0=row0, rows=nb * L).reshape(nb, L, D))
        row0 += nb * L
    return tuple(outs)


def _prepare_weights(w):
    out = dict(w)
    for name in ('ret_w_in', 'ret_w_out', 'hg_w_in', 'hg_w_out', 'mla_w_out', 'gqa_w_in', 'gqa_w_out',
                 'mem_w_q', 'mem_w_kv', 'mem_w_out', 'mlp_w1', 'mlp_w2', 'mla_w_kvb'):
        out[name] = w[name].astype(BF16)
    n = w['mla_w_in'].shape[0]
    lat = MLA_Q_RANK + MLA_KV_RANK
    out['mla_w_in'] = jnp.concatenate([w['mla_w_in'][..., :lat], _spread_pairs(w['mla_w_in'][..., lat:])],
                                      axis=-1).astype(BF16)
    wq = w['mla_w_qb'].reshape(n, MLA_Q_RANK, MLA_HEADS, MLA_NOPE + MLA_ROPE)
    wq = jnp.concatenate([wq[..., :MLA_NOPE], _spread_pairs(wq[..., MLA_NOPE:])], axis=-1)
    out['mla_w_qb'] = wq.reshape(n, MLA_Q_RANK, -1).astype(BF16)
    g = w['mla_qk_norm'].astype(F32)
    g = jnp.concatenate([g[..., :MLA_NOPE], _spread_pairs(g[..., MLA_NOPE:])], axis=-1)
    out['mla_gq'] = g[:, 0:1] * ((MLA_NOPE + MLA_ROPE) ** -0.5 * LOG2_E)
    out['mla_gk'] = g[:, 1:2]
    nqk = GQA_HEADS + GQA_KV_HEADS
    wi = w['gqa_w_in']
    wqk = _interleave_halves(wi[..., :nqk * GQA_HD].reshape(wi.shape[0], wi.shape[1], nqk, GQA_HD))
    out['gqa_w_in'] = jnp.concatenate([wqk.reshape(wi.shape[0], wi.shape[1], -1), wi[..., nqk * GQA_HD:]],
                                      axis=-1).astype(BF16)
    g = _interleave_halves(w['gqa_qk_norm'].astype(F32))
    out['gqa_gains'] = jnp.concatenate([jnp.tile(g[:, 0] * (GQA_HD ** -0.5 * LOG2_E), (1, GQA_HEADS)),
                                        jnp.tile(g[:, 1], (1, GQA_KV_HEADS))], axis=-1)[:, None, :]
    return out


def kernel(x_prompt, x_sample, mem_prompt, mem_sample, norm_mix, norm_mem, norm_memtok, norm_mlp, ret_w_in, ret_decay, ret_out_norm, ret_w_out, hg_w_in, hg_lb, hg_out_norm, hg_w_out, mla_w_in, mla_q_norm, mla_kv_norm, mla_w_qb, mla_w_kvb, mla_qk_norm, mla_w_out, gqa_w_in, gqa_qk_norm, gqa_w_out, mem_w_q, mem_w_kv, mem_qk_norm, mem_w_out, mlp_w1, mlp_w2):
    w = _prepare_weights(dict(
        norm_mix=norm_mix, norm_mem=norm_mem, norm_memtok=norm_memtok, norm_mlp=norm_mlp,
        ret_w_in=ret_w_in, ret_decay=ret_decay, ret_out_norm=ret_out_norm, ret_w_out=ret_w_out,
        hg_w_in=hg_w_in, hg_lb=hg_lb, hg_out_norm=hg_out_norm, hg_w_out=hg_w_out,
        mla_w_in=mla_w_in, mla_q_norm=mla_q_norm, mla_kv_norm=mla_kv_norm, mla_w_qb=mla_w_qb,
        mla_w_kvb=mla_w_kvb, mla_qk_norm=mla_qk_norm, mla_w_out=mla_w_out,
        gqa_w_in=gqa_w_in, gqa_qk_norm=gqa_qk_norm, gqa_w_out=gqa_w_out,
        mem_w_q=mem_w_q, mem_w_kv=mem_w_kv, mem_qk_norm=mem_qk_norm, mem_w_out=mem_w_out,
        mlp_w1=mlp_w1, mlp_w2=mlp_w2))
    mem = jnp.concatenate([mem_prompt, mem_sample], axis=0)
    return _trunk([x_prompt, x_sample], mem, w)
```

```python
import functools

import jax
import jax.numpy as jnp
from jax import lax
from jax.experimental import pallas as pl
from jax.experimental.pallas import tpu as pltpu

F32 = jnp.float32
BF16 = jnp.bfloat16

D_MODEL = 2048
GRID_W = 64
NORM_EPS = 1e-6
LOG2_E = 1.4426950408889634
RET_HEADS, RET_DK, RET_DV = 8, 256, 512
RET_ROPE_BASE = 10000.0
HG_HEADS, HG_DK, HG_DV, HG_CHUNK = 16, 128, 128, 32
MLA_HEADS, MLA_Q_RANK, MLA_KV_RANK, MLA_NOPE, MLA_ROPE, MLA_V = 16, 512, 512, 128, 64, 128
MLA_ROPE_BASE = 10000.0
GQA_HEADS, GQA_KV_HEADS, GQA_HD = 16, 4, 128
GQA_ROPE_BASE = 10000.0
MEM_HEADS, MEM_HD = 4, 128

LANES = 128
VMEM_LIMIT_BYTES = 56 * 2 ** 20

HG_BLOCK = 128
HG_HEADS_PER_STEP = 4
MLA_HEADS_PER_STEP = 4
ATTN_HEADS_PER_STEP = 8


def _params(sem):
    return pltpu.CompilerParams(dimension_semantics=sem, vmem_limit_bytes=VMEM_LIMIT_BYTES)


def _tile(n, pref):
    return pref if n % pref == 0 else n


def _rms_scale(x, width=None):
    width = x.shape[-1] if width is None else width
    ms = jnp.sum(x * x, axis=-1, keepdims=True) * (1.0 / width)
    return x * lax.rsqrt(ms + NORM_EPS)


def _inv_rms_lanes(x, width):
    sq = (x * x).astype(BF16)
    ss = _dot(sq, jnp.ones((x.shape[-1], LANES), BF16))
    return lax.rsqrt(ss * (1.0 / width) + NORM_EPS)


def _sigmoid(x):
    return 1.0 / (1.0 + jnp.exp(-x))


def _dot(a, b):
    return jnp.dot(a, b, preferred_element_type=F32)


def _dot_nt(a, b):
    return lax.dot_general(a, b, (((1,), (1,)), ((), ())), preferred_element_type=F32)


def _rope_pairs(x, c, s):
    return x * c + pltpu.roll(x, LANES // 2, 1) * s


def _row_parts(xs, tm):
    bounds, lo = [], 0
    for x in xs:
        assert x.shape[0] % tm == 0
        bounds.append((lo, x.shape[0] // tm))
        lo += x.shape[0] // tm

    def index_map(part, col):
        lo, n = bounds[part]
        return lambda i, j: (jnp.clip(i - lo, 0, n - 1), jnp.where((i >= lo) & (i < lo + n), col(j), 0))

    return bounds, index_map


def _select_part(bounds, fn):
    i = pl.program_id(0)
    for part, (lo, n) in enumerate(bounds):
        if len(bounds) == 1:
            fn(part)
        else:
            pl.when((i >= lo) & (i < lo + n))(functools.partial(fn, part))


def _norm_matmul_kernel(*refs, bounds):
    x_refs = refs[:len(bounds)]
    g_ref, w_ref, o_ref, xn_ref = refs[len(bounds):]

    def normalise(part):
        x = x_refs[part][...].astype(F32)
        xn_ref[...] = (_rms_scale(x) * g_ref[...]).astype(BF16)

    @pl.when(pl.program_id(1) == 0)
    def _():
        _select_part(bounds, normalise)

    o_ref[...] = _dot(xn_ref[...], w_ref[...]).astype(o_ref.dtype)


def _norm_matmul(xs, g, w, out_dtype, tm=1024, tn=1024):
    xs = xs if isinstance(xs, (list, tuple)) else [xs]
    T, K = sum(x.shape[0] for x in xs), xs[0].shape[1]
    N = w.shape[1]
    tm, tn = _tile(min(x.shape[0] for x in xs), tm), _tile(N, tn)
    bounds, index_map = _row_parts(xs, tm)
    return pl.pallas_call(
        functools.partial(_norm_matmul_kernel, bounds=bounds),
        grid=(T // tm, N // tn),
        in_specs=[pl.BlockSpec((tm, K), index_map(p, lambda j: 0)) for p in range(len(xs))] + [
                  pl.BlockSpec((1, K), lambda i, j: (0, 0)),
                  pl.BlockSpec((K, tn), lambda i, j: (0, j))],
        out_specs=pl.BlockSpec((tm, tn), lambda i, j: (i, j)),
        out_shape=jax.ShapeDtypeStruct((T, N), out_dtype),
        scratch_shapes=[pltpu.VMEM((tm, K), BF16)],
        compiler_params=_params(("parallel", "arbitrary")),
        name="norm_matmul",
    )(*xs, g.reshape(1, K).astype(F32), w)


def _matmul_residual_kernel(a_ref, w_ref, *refs, bounds):
    x_refs, o_ref = refs[:len(bounds)], refs[len(bounds)]
    acc = _dot(a_ref[...], w_ref[...])

    def add(part):
        o_ref[...] = x_refs[part][...] + acc

    _select_part(bounds, add)


def _matmul_residual(a, w, xs, tm=1024, tn=512):
    xs = xs if isinstance(xs, (list, tuple)) else [xs]
    T, K = a.shape
    N = w.shape[1]
    tm, tn = _tile(min(x.shape[0] for x in xs), tm), _tile(N, tn)
    bounds, index_map = _row_parts(xs, tm)
    return pl.pallas_call(
        functools.partial(_matmul_residual_kernel, bounds=bounds),
        grid=(T // tm, N // tn),
        in_specs=[pl.BlockSpec((tm, K), lambda i, j: (i, 0)),
                  pl.BlockSpec((K, tn), lambda i, j: (0, j))] + [
                  pl.BlockSpec((tm, tn), index_map(p, lambda j: j)) for p in range(len(xs))],
        out_specs=pl.BlockSpec((tm, tn), lambda i, j: (i, j)),
        out_shape=jax.ShapeDtypeStruct((T, N), F32),
        input_output_aliases={2: 0} if len(xs) == 1 else {},
        compiler_params=_params(("parallel", "arbitrary")),
        name="matmul_residual",
    )(a, w, *xs)


def _mlp_kernel(x_ref, g_ref, w1_ref, w2_ref, o_ref, xn_ref):
    @pl.when(pl.program_id(1) == 0)
    def _():
        x = x_ref[...]
        xn_ref[...] = (_rms_scale(x) * g_ref[...]).astype(BF16)
        o_ref[...] = x

    a = jnp.maximum(_dot(xn_ref[...], w1_ref[...]), 0.0)
    o_ref[...] += _dot((a * a).astype(BF16), w2_ref[...])


def _mlp(x, g, w1, w2, layer, row0=0, rows=None, tm=512, tf=1024):
    D = x.shape[1]
    rows = x.shape[0] if rows is None else rows
    Fd = w1.shape[2]
    tm, tf = _tile(rows, tm), _tile(Fd, tf)
    assert row0 % tm == 0
    blk0 = row0 // tm
    return pl.pallas_call(
        _mlp_kernel,
        grid=(rows // tm, Fd // tf),
        in_specs=[pl.BlockSpec((tm, D), lambda i, f: (blk0 + i, 0)),
                  pl.BlockSpec((1, D), lambda i, f: (0, 0)),
                  pl.BlockSpec((None, D, tf), lambda i, f: (layer, 0, f)),
                  pl.BlockSpec((None, tf, D), lambda i, f: (layer, f, 0))],
        out_specs=pl.BlockSpec((tm, D), lambda i, f: (i, 0)),
        out_shape=jax.ShapeDtypeStruct((rows, D), F32),
        scratch_shapes=[pltpu.VMEM((tm, D), BF16)],
        compiler_params=_params(("parallel", "arbitrary")),
        name="mlp",
    )(x, g.reshape(1, D).astype(F32), w1, w2)


def _xattn_kernel(x_ref, g_ref, wq_ref, kv_ref, gq_ref, gk_ref, wo_ref, o_ref):
    x = x_ref[0]
    xn = (_rms_scale(x) * g_ref[...]).astype(BF16)
    q = _dot(xn, wq_ref[...])
    kv = kv_ref[0].astype(F32)
    width = MEM_HEADS * MEM_HD
    outs = []
    for h in range(MEM_HEADS):
        cs = slice(h * MEM_HD, (h + 1) * MEM_HD)
        qh = (_rms_scale(q[:, cs]) * gq_ref[...]).astype(BF16)
        kh = (_rms_scale(kv[:, cs]) * gk_ref[...]).astype(BF16)
        vh = kv_ref[0, :, width + h * MEM_HD: width + (h + 1) * MEM_HD]
        s = _dot_nt(qh, kh)
        p = jnp.exp(s - jnp.max(s, axis=-1, keepdims=True))
        l = jnp.sum(p, axis=-1, keepdims=True)
        outs.append((_dot(p.astype(BF16), vh) / l).astype(BF16))
    o = jnp.concatenate(outs, axis=-1)
    o_ref[0] = x + _dot(o, wo_ref[...])


def _xattn(x, kv, g, wq, gq, gk, wo, tm=1024):
    B, L, D = x.shape
    M = kv.shape[1]
    width = MEM_HEADS * MEM_HD
    tm = _tile(L, tm)
    return pl.pallas_call(
        _xattn_kernel,
        grid=(B, L // tm),
        in_specs=[pl.BlockSpec((1, tm, D), lambda b, i: (b, i, 0)),
                  pl.BlockSpec((1, D), lambda b, i: (0, 0)),
                  pl.BlockSpec((D, width), lambda b, i: (0, 0)),
                  pl.BlockSpec((1, M, 2 * width), lambda b, i: (b, 0, 0)),
                  pl.BlockSpec((1, MEM_HD), lambda b, i: (0, 0)),
                  pl.BlockSpec((1, MEM_HD), lambda b, i: (0, 0)),
                  pl.BlockSpec((width, D), lambda b, i: (0, 0))],
        out_specs=pl.BlockSpec((1, tm, D), lambda b, i: (b, i, 0)),
        out_shape=jax.ShapeDtypeStruct((B, L, D), F32),
        input_output_aliases={0: 0},
        compiler_params=_params(("parallel", "arbitrary")),
        name="mem_xattn",
    )(x, g.reshape(1, D).astype(F32), wq, kv,
      (gq.astype(F32) * (MEM_HD ** -0.5)).reshape(1, MEM_HD), gk.reshape(1, MEM_HD).astype(F32), wo)


def _attn_kernel(q_ref, k_ref, v_ref, o_ref, *, heads, q_per_kv, dq, dv):
    def scores(g):
        kg = g // q_per_kv
        return _dot_nt(q_ref[0, :, g * dq:(g + 1) * dq], k_ref[0, :, kg * dq:(kg + 1) * dq])

    s_next = scores(0)
    for g in range(heads):
        s, s_next = s_next, (scores(g + 1) if g + 1 < heads else None)
        kg = g // q_per_kv
        v = v_ref[0, :, kg * dv:(kg + 1) * dv]
        p = jnp.exp2(s - jnp.max(s, axis=-1, keepdims=True))
        l = jnp.sum(p, axis=-1, keepdims=True)
        o_ref[0, :, g * dv:(g + 1) * dv] = (_dot(p.astype(BF16), v) / l).astype(o_ref.dtype)


def _attention(q, k, v, *, n_heads, q_per_kv, dq, dv, k_col0, v_col0, tq=512):
    B, L = q.shape[0], q.shape[1]
    tq = _tile(L, tq)
    heads = ATTN_HEADS_PER_STEP
    kvh = heads // q_per_kv
    k_block0, v_block0 = k_col0 // (kvh * dq), v_col0 // (kvh * dv)
    assert n_heads % heads == 0 and k_col0 % (kvh * dq) == 0 and v_col0 % (kvh * dv) == 0
    return pl.pallas_call(
        functools.partial(_attn_kernel, heads=heads, q_per_kv=q_per_kv, dq=dq, dv=dv),
        grid=(B, n_heads // heads, L // tq),
        in_specs=[pl.BlockSpec((1, tq, heads * dq), lambda b, h, i: (b, i, h)),
                  pl.BlockSpec((1, L, kvh * dq), lambda b, h, i: (b, 0, k_block0 + h)),
                  pl.BlockSpec((1, L, kvh * dv), lambda b, h, i: (b, 0, v_block0 + h))],
        out_specs=pl.BlockSpec((1, tq, heads * dv), lambda b, h, i: (b, i, h)),
        out_shape=jax.ShapeDtypeStruct((B, L, n_heads * dv), BF16),
        compiler_params=_params(("parallel", "parallel", "arbitrary")),
        name="softmax_attention",
    )(q, k, v)


def _retention_kernel(q_ref, k_ref, v_ref, g_ref, cos_ref, sin_ref, dec_ref, on_ref, o_ref, *, tq):
    L = q_ref.shape[1]
    half = RET_DK // 2
    cos = cos_ref[...]
    sin = sin_ref[...]

    def rope(ref, scale):
        x1 = ref[0, :, :half].astype(F32)
        x2 = ref[0, :, half:].astype(F32)
        y = jnp.concatenate([x1 * cos - x2 * sin, x1 * sin + x2 * cos], axis=-1)
        return (y * scale).astype(BF16)

    q = rope(q_ref, RET_DK ** -0.5)
    k = rope(k_ref, 1.0)
    v = v_ref[0]
    dec = dec_ref[0]
    lg = jnp.minimum(dec, 0.0) - jnp.log1p(jnp.exp(-jnp.abs(dec)))
    lg_f = lg[0:1, 0:1]
    lg_b = lg[1:2, 0:1]
    n = L // tq
    ri = lax.broadcasted_iota(jnp.int32, (tq, tq), 0)
    diff = (ri - lax.broadcasted_iota(jnp.int32, (tq, tq), 1)).astype(F32)
    d_intra = jnp.where(diff == 0, 2.0, jnp.exp(jnp.where(diff >= 0, diff * lg_f, -diff * lg_b)))
    r = lax.broadcasted_iota(jnp.int32, (tq, RET_DK), 0).astype(F32)
    qdec = (jnp.exp((r + 1.0) * lg_f), jnp.exp((tq - r) * lg_b))
    kdec = (jnp.exp((tq - 1.0 - r) * lg_f), jnp.exp(r * lg_b))
    gc = (jnp.exp(tq * lg_f), jnp.exp(tq * lg_b))
    rows = [slice(i * tq, (i + 1) * tq) for i in range(n)]

    def kv_state(m, d):
        kd = (k[rows[m]].astype(F32) * kdec[d]).T.astype(BF16)
        return _dot(kd, v[rows[m]])

    state = [[None] * n, [None] * n]
    for i in range(1, n):
        prev = state[0][i - 1]
        state[0][i] = kv_state(i - 1, 0) if prev is None else prev * gc[0] + kv_state(i - 1, 0)
    for i in range(n - 2, -1, -1):
        nxt = state[1][i + 1]
        state[1][i] = kv_state(i + 1, 1) if nxt is None else nxt * gc[1] + kv_state(i + 1, 1)
    for i in range(n):
        qi = q[rows[i]]
        s = _dot_nt(qi, k[rows[i]]) * d_intra
        o = _dot(s.astype(BF16), v[rows[i]])
        for d in (0, 1):
            if state[d][i] is not None:
                o = o + _dot((qi.astype(F32) * qdec[d]).astype(BF16), state[d][i].astype(BF16))
        y = _rms_scale(o) * on_ref[...]
        gate = g_ref[0, rows[i], :].astype(F32)
        o_ref[0, rows[i], :] = (y * (gate * _sigmoid(gate))).astype(o_ref.dtype)


def _retention(p, cos, sin, dec, out_norm, tq=512):
    B, L, _ = p.shape
    H = RET_HEADS
    tq = _tile(L, tq)
    v_block0 = 2 * H * RET_DK // RET_DV
    return pl.pallas_call(
        functools.partial(_retention_kernel, tq=tq),
        grid=(H, B),
        in_specs=[pl.BlockSpec((1, L, RET_DK), lambda h, b: (b, 0, h)),
                  pl.BlockSpec((1, L, RET_DK), lambda h, b: (b, 0, H + h)),
                  pl.BlockSpec((1, L, RET_DV), lambda h, b: (b, 0, v_block0 + h)),
                  pl.BlockSpec((1, L, RET_DV), lambda h, b: (b, 0, v_block0 + H + h)),
                  pl.BlockSpec((L, RET_DK // 2), lambda h, b: (0, 0)),
                  pl.BlockSpec((L, RET_DK // 2), lambda h, b: (0, 0)),
                  pl.BlockSpec((1, 2, LANES), lambda h, b: (h, 0, 0)),
                  pl.BlockSpec((1, RET_DV), lambda h, b: (0, 0))],
        out_specs=pl.BlockSpec((1, L, RET_DV), lambda h, b: (b, 0, h)),
        out_shape=jax.ShapeDtypeStruct((B, L, H * RET_DV), BF16),
        compiler_params=_params(("parallel", "arbitrary")),
        name="retention",
    )(p, p, p, p, cos, sin, dec, out_norm.reshape(1, RET_DV).astype(F32))


def _hgrn_kernel(q_ref, ff_ref, fb_ref, i_ref, g_ref, lb_ref, on_ref, m_ref, o_ref,
                 of_ref, ob_ref, st_ref, qt_ref, kt_ref, ks_ref, dec_ref, *, heads):
    L = q_ref.shape[1]
    nblk = L // HG_BLOCK
    n_chunks = HG_BLOCK // HG_CHUNK
    width = heads * HG_DK
    rows = lax.broadcasted_iota(jnp.int32, (HG_BLOCK, HG_BLOCK), 0)
    cols = lax.broadcasted_iota(jnp.int32, (HG_BLOCK, HG_BLOCK), 1)
    same_chunk = (rows // HG_CHUNK) == (cols // HG_CHUNK)
    masks = (same_chunk & (cols <= rows), same_chunk & (cols >= rows))
    row_chunk = lax.broadcasted_iota(jnp.int32, (HG_BLOCK, HG_DK), 0) // HG_CHUNK
    st_ref[...] = jnp.zeros_like(st_ref)

    def block_rows(t, d):
        r = t if d == 0 else nblk - 1 - t
        return pl.ds(pl.multiple_of(r * HG_BLOCK, HG_BLOCK), HG_BLOCK)

    def prep(t, d, slot):
        rs = block_rows(t, d)
        fz = (ff_ref if d == 0 else fb_ref)[0, rs, :].astype(F32)
        lb = lb_ref[...]
        f = lb + (1.0 - lb) * _sigmoid(fz)
        lf = jnp.log(f)
        kk = 1.0 - f
        hi = lf.astype(BF16)
        lo = (lf - hi.astype(F32)).astype(BF16)
        cb = _dot(m_ref[d], jnp.concatenate([hi, lo], axis=1))
        b = cb[:, :width] + cb[:, width:]
        last = [c * HG_CHUNK + (HG_CHUNK - 1 if d == 0 else 0) for c in range(n_chunks)]
        dec_rows = [jnp.exp(b[i:i + 1, :]) for i in last]
        dec = jnp.concatenate([jnp.broadcast_to(row, (HG_CHUNK, width)) for row in dec_rows], axis=0)
        kt = kk * jnp.exp(-b)
        qt_ref[slot, d] = (q_ref[0, rs, :].astype(F32) * jnp.exp(b)).astype(BF16)
        kt_ref[slot, d] = kt.astype(BF16)
        ks_ref[slot, d] = (kt * dec).astype(BF16)
        for c in range(n_chunks):
            dec_ref[slot, d, c:c + 1, :] = dec_rows[c]

    items = [(h, d) for h in range(heads) for d in (0, 1)]

    def half(t, slot):
        rs = [block_rows(t, d) for d in (0, 1)]
        v = {(h, d): i_ref[0, rs[d], h * HG_DK:(h + 1) * HG_DK] for h, d in items}
        qt = {(h, d): qt_ref[slot, d, :, h * HG_DK:(h + 1) * HG_DK] for h, d in items}
        s = {(h, d): _dot_nt(qt[h, d], kt_ref[slot, d, :, h * HG_DK:(h + 1) * HG_DK]) for h, d in items}
        kv = {}
        for h, d in items:
            ks = ks_ref[slot, d, :, h * HG_DK:(h + 1) * HG_DK].astype(F32)
            ks_cols = jnp.concatenate([jnp.where(row_chunk == c, ks, 0.0) for c in range(n_chunks)], axis=1)
            kv[h, d] = _dot(v[h, d].astype(F32).T.astype(BF16), ks_cols.astype(BF16))
        t_next = jnp.minimum(t + 1, nblk - 1)
        prep(t_next, 0, 1 - slot)
        s = {it: jnp.where(masks[it[1]], s[it], 0.0).astype(BF16) for it in items}
        o = {it: _dot(s[it], v[it]) for it in items}
        st = {(h, d): st_ref[2 * h + d] for h, d in items}
        parts = {it: [None] * n_chunks for it in items}
        for step in range(n_chunks):
            if step == n_chunks // 2:
                prep(t_next, 1, 1 - slot)
            for h, d in items:
                c = step if d == 0 else n_chunks - 1 - step
                cr = slice(c * HG_CHUNK, (c + 1) * HG_CHUNK)
                parts[h, d][c] = o[h, d][cr] + _dot_nt(qt[h, d][cr], st[h, d].astype(BF16))
                dec = dec_ref[slot, d, c:c + 1, h * HG_DK:(h + 1) * HG_DK]
                st[h, d] = st[h, d] * dec + kv[h, d][:, c * HG_DK:(c + 1) * HG_DK]
        for h, d in items:
            st_ref[2 * h + d] = st[h, d]
        for d in (0, 1):
            out = jnp.concatenate([jnp.concatenate(parts[h, d], axis=0) for h in range(heads)], axis=1)
            (of_ref if d == 0 else ob_ref)[rs[d], :] = out

    def body(u, carry):
        half(2 * u, 0)
        half(2 * u + 1, 1)
        return carry

    for d in (0, 1):
        prep(jnp.int32(0), d, 0)
    lax.fori_loop(0, nblk // 2, body, 0)
    for h in range(heads):
        cs = slice(h * HG_DV, (h + 1) * HG_DV)
        y = _rms_scale(of_ref[:, cs] + ob_ref[:, cs]) * on_ref[...]
        gate = g_ref[0, :, cs].astype(F32)
        o_ref[0, :, cs] = (y * (gate * _sigmoid(gate))).astype(o_ref.dtype)


def _hgrn(p, lb, out_norm):
    B, L, _ = p.shape
    heads = HG_HEADS_PER_STEP
    nb = HG_HEADS // heads
    w = heads * HG_DK
    c = HG_CHUNK
    n_chunks = HG_BLOCK // HG_CHUNK
    assert (L // HG_BLOCK) % 2 == 0
    idx = jnp.arange(HG_BLOCK)
    same = (idx[:, None] // c) == (idx[None, :] // c)
    tri_f = same & (idx[None, :] <= idx[:, None])
    tri_b = same & (idx[None, :] >= idx[:, None])
    m = jnp.stack([tri_f, tri_b]).astype(BF16)
    spec = lambda off: pl.BlockSpec((1, L, w), lambda b, j: (b, 0, off * nb + j))
    return pl.pallas_call(
        functools.partial(_hgrn_kernel, heads=heads),
        grid=(B, nb),
        in_specs=[spec(0), spec(1), spec(2), spec(3), spec(4),
                  pl.BlockSpec((1, w), lambda b, j: (0, j)),
                  pl.BlockSpec((1, HG_DV), lambda b, j: (0, 0)),
                  pl.BlockSpec((2, HG_BLOCK, HG_BLOCK), lambda b, j: (0, 0, 0))],
        out_specs=pl.BlockSpec((1, L, w), lambda b, j: (b, 0, j)),
        out_shape=jax.ShapeDtypeStruct((B, L, HG_HEADS * HG_DV), BF16),
        scratch_shapes=[pltpu.VMEM((L, w), F32), pltpu.VMEM((L, w), F32),
                        pltpu.VMEM((2 * heads, HG_DV, HG_DK), F32),
                        pltpu.VMEM((2, 2, HG_BLOCK, w), BF16), pltpu.VMEM((2, 2, HG_BLOCK, w), BF16),
                        pltpu.VMEM((2, 2, HG_BLOCK, w), BF16), pltpu.VMEM((2, 2, n_chunks, w), F32)],
        compiler_params=_params(("parallel", "arbitrary")),
        name="hgrn2",
    )(p, p, p, p, p, lb.reshape(1, -1).astype(F32), out_norm.reshape(1, HG_DV).astype(F32), m)


def _mla_q_kernel(c_ref, gn_ref, w_ref, gh_ref, cos_ref, sin_ref, o_ref, xn_ref):
    @pl.when(pl.program_id(1) == 0)
    def _():
        xn_ref[...] = (_rms_scale(c_ref[...]) * gn_ref[...]).astype(BF16)

    hw = 2 * LANES
    head_dot = lambda g: _dot(xn_ref[...], w_ref[:, g * hw:(g + 1) * hw])
    y_next = head_dot(0)
    for g in range(MLA_HEADS_PER_STEP):
        y, y_next = y_next, (head_dot(g + 1) if g + 1 < MLA_HEADS_PER_STEP else None)
        inv = _inv_rms_lanes(y, MLA_NOPE + MLA_ROPE)
        gh = gh_ref[...]
        o_ref[:, g * hw:g * hw + MLA_NOPE] = (y[:, :MLA_NOPE] * inv * gh[:, :MLA_NOPE]).astype(o_ref.dtype)
        o_ref[:, g * hw + MLA_NOPE:(g + 1) * hw] = _rope_pairs(
            y[:, MLA_NOPE:] * inv * gh[:, MLA_NOPE:], cos_ref[...], sin_ref[...]).astype(o_ref.dtype)


def _mla_kv_kernel(c_ref, kr_ref, gn_ref, w_ref, gh_ref, cos_ref, sin_ref, k_ref, v_ref, xn_ref):
    @pl.when(pl.program_id(1) == 0)
    def _():
        xn_ref[...] = (_rms_scale(c_ref[...]) * gn_ref[...]).astype(BF16)

    hw = 2 * LANES
    kr = kr_ref[...]
    gh = gh_ref[...]
    head_dot = lambda g: _dot(xn_ref[...], w_ref[:, g * hw:(g + 1) * hw])
    y_next = head_dot(0)
    for g in range(MLA_HEADS_PER_STEP):
        y, y_next = y_next, (head_dot(g + 1) if g + 1 < MLA_HEADS_PER_STEP else None)
        kn = y[:, :MLA_NOPE]
        inv = _inv_rms_lanes(jnp.concatenate([kn, kr], axis=1), MLA_NOPE + MLA_ROPE)
        k_ref[:, g * hw:g * hw + MLA_NOPE] = (kn * inv * gh[:, :MLA_NOPE]).astype(k_ref.dtype)
        k_ref[:, g * hw + MLA_NOPE:(g + 1) * hw] = _rope_pairs(
            kr * inv * gh[:, MLA_NOPE:], cos_ref[...], sin_ref[...]).astype(k_ref.dtype)
        v_ref[:, g * MLA_V:(g + 1) * MLA_V] = y[:, MLA_NOPE:].astype(v_ref.dtype)


def _mla_qkv(c, L, q_norm, kv_norm, wq, wkv, gq, gk, tabs, tm=1024):
    T = c.shape[0]
    H = MLA_HEADS
    hps = MLA_HEADS_PER_STEP
    tm = _tile(L, tm)
    lt = L // tm
    hw = 2 * LANES
    tab_spec = pl.BlockSpec((tm, LANES), lambda i, h: (i % lt, 0))
    row = lambda n: pl.BlockSpec((1, n), lambda i, h: (0, 0))
    q = pl.pallas_call(
        _mla_q_kernel,
        grid=(T // tm, H // hps),
        in_specs=[pl.BlockSpec((tm, MLA_Q_RANK), lambda i, h: (i, 0)), row(MLA_Q_RANK),
                  pl.BlockSpec((MLA_Q_RANK, hps * hw), lambda i, h: (0, h)), row(hw),
                  tab_spec, tab_spec],
        out_specs=pl.BlockSpec((tm, hps * hw), lambda i, h: (i, h)),
        out_shape=jax.ShapeDtypeStruct((T, H * hw), BF16),
        scratch_shapes=[pltpu.VMEM((tm, MLA_Q_RANK), BF16)],
        compiler_params=_params(("parallel", "arbitrary")),
        name="mla_q",
    )(c, q_norm.reshape(1, -1).astype(F32), wq, gq, *tabs)
    k, v = pl.pallas_call(
        _mla_kv_kernel,
        grid=(T // tm, H // hps),
        in_specs=[pl.BlockSpec((tm, MLA_KV_RANK), lambda i, h: (i, 1)),
                  pl.BlockSpec((tm, LANES), lambda i, h: (i, (MLA_Q_RANK + MLA_KV_RANK) // LANES)),
                  row(MLA_KV_RANK),
                  pl.BlockSpec((MLA_KV_RANK, hps * hw), lambda i, h: (0, h)), row(hw),
                  tab_spec, tab_spec],
        out_specs=[pl.BlockSpec((tm, hps * hw), lambda i, h: (i, h)),
                   pl.BlockSpec((tm, hps * MLA_V), lambda i, h: (i, h))],
        out_shape=[jax.ShapeDtypeStruct((T, H * hw), BF16), jax.ShapeDtypeStruct((T, H * MLA_V), BF16)],
        scratch_shapes=[pltpu.VMEM((tm, MLA_KV_RANK), BF16)],
        compiler_params=_params(("parallel", "arbitrary")),
        name="mla_kv",
    )(c, c, kv_norm.reshape(1, -1).astype(F32), wkv, gk, *tabs)
    return q, k, v


def _head_norm_rope_kernel(x_ref, g_ref, cos_ref, sin_ref, o_ref, *, n_heads):
    for h in range(n_heads):
        cs = slice(h * LANES, (h + 1) * LANES)
        x = x_ref[:, cs].astype(F32)
        y = x * _inv_rms_lanes(x, LANES) * g_ref[:, cs]
        o_ref[:, cs] = _rope_pairs(y, cos_ref[...], sin_ref[...]).astype(o_ref.dtype)


def _head_norm_rope(p, L, n_heads, gains, tabs, tm=512):
    T = p.shape[0]
    tm = _tile(L, tm)
    lt = L // tm
    w = n_heads * LANES
    tab_spec = pl.BlockSpec((tm, LANES), lambda i: (i % lt, 0))
    return pl.pallas_call(
        functools.partial(_head_norm_rope_kernel, n_heads=n_heads),
        grid=(T // tm,),
        in_specs=[pl.BlockSpec((tm, w), lambda i: (i, 0)),
                  pl.BlockSpec((1, w), lambda i: (0, 0)),
                  tab_spec, tab_spec],
        out_specs=pl.BlockSpec((tm, w), lambda i: (i, 0)),
        out_shape=jax.ShapeDtypeStruct((T, w), BF16),
        compiler_params=_params(("parallel",)),
        name="head_norm_rope",
    )(p, gains, *tabs)


ROPE_HALF = 32


def _rope_tables(pos_a, pos_b, base):
    freqs = base ** (-jnp.arange(ROPE_HALF, dtype=F32) / ROPE_HALF)

    def cs(pos):
        ang = pos[:, None] * freqs[None, :]
        return jnp.cos(ang), jnp.sin(ang)

    ca, sa = cs(pos_a)
    cb, sb = (jnp.zeros_like(ca), jnp.zeros_like(sa)) if pos_b is None else cs(pos_b)
    return jnp.concatenate([ca, cb, ca, cb], axis=-1), jnp.concatenate([-sa, -sb, sa, sb], axis=-1)


def _spread_pairs(a):
    z = jnp.zeros(a.shape[:-1] + (ROPE_HALF,), a.dtype)
    return jnp.concatenate([a[..., :ROPE_HALF], z, a[..., ROPE_HALF:], z], axis=-1)


def _interleave_halves(a):
    q = [a[..., i * ROPE_HALF:(i + 1) * ROPE_HALF] for i in range(4)]
    return jnp.concatenate([q[0], q[2], q[1], q[3]], axis=-1)


def _trunk(xs, mem, w):
    group_batches = [x.shape[0] for x in xs]
    B, (L, D) = sum(group_batches), xs[0].shape[1:]
    T = B * L
    M = mem.shape[1]
    x = [x.reshape(-1, D) for x in xs]
    mem = mem.reshape(B * M, D)
    pos = jnp.arange(L, dtype=F32)
    sm = jax.nn.softmax(w['hg_lb'].astype(F32), axis=0)
    lb_all = jnp.cumsum(sm, axis=0) - sm[0]
    depth = w['norm_mix'].shape[0]
    for i in range(depth):
        kind, j = i % 4, i // 4
        if kind == 0:
            p = _norm_matmul(x, w['norm_mix'][i], w['ret_w_in'][j], BF16)
            half = RET_DK // 2
            ang = pos[:, None] * (RET_ROPE_BASE ** (-jnp.arange(half, dtype=F32) / half))[None, :]
            dec = jnp.broadcast_to(w['ret_decay'][j].astype(F32).T[:, :, None], (RET_HEADS, 2, LANES))
            o = _retention(p.reshape(B, L, -1), jnp.cos(ang), jnp.sin(ang), dec, w['ret_out_norm'][j])
            x = _matmul_residual(o.reshape(T, -1), w['ret_w_out'][j], x)
        elif kind == 1:
            p = _norm_matmul(x, w['norm_mix'][i], w['hg_w_in'][j], BF16, tn=2048)
            o = _hgrn(p.reshape(B, L, -1), lb_all[i], w['hg_out_norm'][j])
            x = _matmul_residual(o.reshape(T, -1), w['hg_w_out'][j], x, tm=512, tn=D)
        elif kind == 2:
            c = _norm_matmul(x, w['norm_mix'][i], w['mla_w_in'][j], F32)
            tabs = _rope_tables(pos, None, MLA_ROPE_BASE)
            q, k, v = _mla_qkv(c, L, w['mla_q_norm'][j], w['mla_kv_norm'][j], w['mla_w_qb'][j], w['mla_w_kvb'][j],
                               w['mla_gq'][j], w['mla_gk'][j], tabs)
            o = _attention(q.reshape(B, L, -1), k.reshape(B, L, -1), v.reshape(B, L, -1),
                           n_heads=MLA_HEADS, q_per_kv=1, dq=2 * LANES, dv=MLA_V, k_col0=0, v_col0=0)
            x = _matmul_residual(o.reshape(T, -1), w['mla_w_out'][j], x, tm=512, tn=D)
        else:
            p = _norm_matmul(x, w['norm_mix'][i], w['gqa_w_in'][j], BF16)
            t = jnp.arange(L)
            tabs = _rope_tables((t // GRID_W).astype(F32), (t % GRID_W).astype(F32), GQA_ROPE_BASE)
            nqk = GQA_HEADS + GQA_KV_HEADS
            qk = _head_norm_rope(p, L, nqk, w['gqa_gains'][j], tabs)
            o = _attention(qk.reshape(B, L, -1), qk.reshape(B, L, -1), p.reshape(B, L, -1),
                           n_heads=GQA_HEADS, q_per_kv=GQA_HEADS // GQA_KV_HEADS, dq=GQA_HD, dv=GQA_HD,
                           k_col0=GQA_HEADS * GQA_HD, v_col0=nqk * GQA_HD)
            x = _matmul_residual(o.reshape(T, -1), w['gqa_w_out'][j], x, tm=512, tn=D)
        kv = _norm_matmul(mem, w['norm_memtok'][i], w['mem_w_kv'][i], BF16)
        x = _xattn(x.reshape(B, L, D), kv.reshape(B, M, -1), w['norm_mem'][i], w['mem_w_q'][i],
                   w['mem_qk_norm'][i, 0], w['mem_qk_norm'][i, 1], w['mem_w_out'][i]).reshape(T, D)
        mlp_w = (w['norm_mlp'][i], w['mlp_w1'], w['mlp_w2'], i)
        if i < depth - 1:
            x = _mlp(x, *mlp_w)
    outs = []
    row0 = 0
    for nb in group_batches:
        outs.append(_mlp(x, *mlp_w, row0=row0, rows=nb * L).reshape(nb, L, D))
        row0 += nb * L
    return tuple(outs)


def _prepare_weights(w):
    out = dict(w)
    for name in ('ret_w_in', 'ret_w_out', 'hg_w_in', 'hg_w_out', 'mla_w_out', 'gqa_w_in', 'gqa_w_out',
                 'mem_w_q', 'mem_w_kv', 'mem_w_out', 'mlp_w1', 'mlp_w2', 'mla_w_kvb'):
        out[name] = w[name].astype(BF16)
    n = w['mla_w_in'].shape[0]
    lat = MLA_Q_RANK + MLA_KV_RANK
    out['mla_w_in'] = jnp.concatenate([w['mla_w_in'][..., :lat], _spread_pairs(w['mla_w_in'][..., lat:])],
                                      axis=-1).astype(BF16)
    wq = w['mla_w_qb'].reshape(n, MLA_Q_RANK, MLA_HEADS, MLA_NOPE + MLA_ROPE)
    wq = jnp.concatenate([wq[..., :MLA_NOPE], _spread_pairs(wq[..., MLA_NOPE:])], axis=-1)
    out['mla_w_qb'] = wq.reshape(n, MLA_Q_RANK, -1).astype(BF16)
    g = w['mla_qk_norm'].astype(F32)
    g = jnp.concatenate([g[..., :MLA_NOPE], _spread_pairs(g[..., MLA_NOPE:])], axis=-1)
    out['mla_gq'] = g[:, 0:1] * ((MLA_NOPE + MLA_ROPE) ** -0.5 * LOG2_E)
    out['mla_gk'] = g[:, 1:2]
    nqk = GQA_HEADS + GQA_KV_HEADS
    wi = w['gqa_w_in']
    wqk = _interleave_halves(wi[..., :nqk * GQA_HD].reshape(wi.shape[0], wi.shape[1], nqk, GQA_HD))
    out['gqa_w_in'] = jnp.concatenate([wqk.reshape(wi.shape[0], wi.shape[1], -1), wi[..., nqk * GQA_HD:]],
                                      axis=-1).astype(BF16)
    g = _interleave_halves(w['gqa_qk_norm'].astype(F32))
    out['gqa_gains'] = jnp.concatenate([jnp.tile(g[:, 0] * (GQA_HD ** -0.5 * LOG2_E), (1, GQA_HEADS)),
                                        jnp.tile(g[:, 1], (1, GQA_KV_HEADS))], axis=-1)[:, None, :]
    return out


def kernel(x_prompt, x_sample, mem_prompt, mem_sample, norm_mix, norm_mem, norm_memtok, norm_mlp, ret_w_in, ret_decay, ret_out_norm, ret_w_out, hg_w_in, hg_lb, hg_out_norm, hg_w_out, mla_w_in, mla_q_norm, mla_kv_norm, mla_w_qb, mla_w_kvb, mla_qk_norm, mla_w_out, gqa_w_in, gqa_qk_norm, gqa_w_out, mem_w_q, mem_w_kv, mem_qk_norm, mem_w_out, mlp_w1, mlp_w2):
    w = _prepare_weights(dict(
        norm_mix=norm_mix, norm_mem=norm_mem, norm_memtok=norm_memtok, norm_mlp=norm_mlp,
        ret_w_in=ret_w_in, ret_decay=ret_decay, ret_out_norm=ret_out_norm, ret_w_out=ret_w_out,
        hg_w_in=hg_w_in, hg_lb=hg_lb, hg_out_norm=hg_out_norm, hg_w_out=hg_w_out,
        mla_w_in=mla_w_in, mla_q_norm=mla_q_norm, mla_kv_norm=mla_kv_norm, mla_w_qb=mla_w_qb,
        mla_w_kvb=mla_w_kvb, mla_qk_norm=mla_qk_norm, mla_w_out=mla_w_out,
        gqa_w_in=gqa_w_in, gqa_qk_norm=gqa_qk_norm, gqa_w_out=gqa_w_out,
        mem_w_q=mem_w_q, mem_w_kv=mem_w_kv, mem_qk_norm=mem_qk_norm, mem_w_out=mem_w_out,
        mlp_w1=mlp_w1, mlp_w2=mlp_w2))
    mem = jnp.concatenate([mem_prompt, mem_sample], axis=0)
    return _trunk([x_prompt, x_sample], mem, w)
```

```python
import functools

import jax
import jax.numpy as jnp
from jax import lax
from jax.experimental import pallas as pl
from jax.experimental.pallas import tpu as pltpu

F32 = jnp.float32
BF16 = jnp.bfloat16

D_MODEL = 2048
GRID_W = 64
NORM_EPS = 1e-6
LOG2_E = 1.4426950408889634
RET_HEADS, RET_DK, RET_DV = 8, 256, 512
RET_ROPE_BASE = 10000.0
HG_HEADS, HG_DK, HG_DV, HG_CHUNK = 16, 128, 128, 32
MLA_HEADS, MLA_Q_RANK, MLA_KV_RANK, MLA_NOPE, MLA_ROPE, MLA_V = 16, 512, 512, 128, 64, 128
MLA_ROPE_BASE = 10000.0
GQA_HEADS, GQA_KV_HEADS, GQA_HD = 16, 4, 128
GQA_ROPE_BASE = 10000.0
MEM_HEADS, MEM_HD = 4, 128

LANES = 128
VMEM_LIMIT_BYTES = 56 * 2 ** 20

HG_BLOCK = 128
HG_HEADS_PER_STEP = 4
MLA_HEADS_PER_STEP = 8
ATTN_HEADS_PER_STEP = 8
RET_HEADS_PER_STEP = 1
MLP_FF_CHUNK = 1024


def _params(sem):
    return pltpu.CompilerParams(dimension_semantics=sem, vmem_limit_bytes=VMEM_LIMIT_BYTES)


def _tile(n, pref):
    return pref if n % pref == 0 else n


def _rms_scale(x, width=None):
    width = x.shape[-1] if width is None else width
    ms = jnp.sum(x * x, axis=-1, keepdims=True) * (1.0 / width)
    return x * lax.rsqrt(ms + NORM_EPS)


def _inv_rms_lanes(x, width):
    sq = (x * x).astype(BF16)
    ss = _dot(sq, jnp.ones((x.shape[-1], LANES), BF16))
    return lax.rsqrt(ss * (1.0 / width) + NORM_EPS)


def _sigmoid(x):
    return 1.0 / (1.0 + jnp.exp(-x))


def _dot(a, b):
    return jnp.dot(a, b, preferred_element_type=F32)


def _dot_nt(a, b):
    return lax.dot_general(a, b, (((1,), (1,)), ((), ())), preferred_element_type=F32)


def _rope_pairs(x, c, s):
    return x * c + pltpu.roll(x, LANES // 2, 1) * s


def _row_parts(xs, tm):
    bounds, lo = [], 0
    for x in xs:
        assert x.shape[0] % tm == 0
        bounds.append((lo, x.shape[0] // tm))
        lo += x.shape[0] // tm

    def index_map(part, col):
        lo, n = bounds[part]
        return lambda i, j: (jnp.clip(i - lo, 0, n - 1), jnp.where((i >= lo) & (i < lo + n), col(j), 0))

    return bounds, index_map


def _read_part(bounds, x_refs):
    i = pl.program_id(0)
    x = x_refs[-1][...]
    for part in range(len(bounds) - 2, -1, -1):
        lo, n = bounds[part]
        x = jnp.where((i >= lo) & (i < lo + n), x_refs[part][...], x)
    return x


def _norm_matmul_kernel(*refs, bounds):
    x_refs = refs[:len(bounds)]
    g_ref, w_ref, o_ref, xn_ref = refs[len(bounds):]

    def normalise(part):
        x = x_refs[part][...].astype(F32)
        xn_ref[...] = (_rms_scale(x) * g_ref[...]).astype(BF16)

    @pl.when(pl.program_id(1) == 0)
    def _():
        i = pl.program_id(0)
        for part, (lo, n) in enumerate(bounds):
            if len(bounds) == 1:
                normalise(part)
            else:
                pl.when((i >= lo) & (i < lo + n))(functools.partial(normalise, part))

    o_ref[...] = _dot(xn_ref[...], w_ref[...]).astype(o_ref.dtype)


def _norm_matmul(xs, g, w, out_dtype, tm=1024, tn=1024):
    xs = xs if isinstance(xs, (list, tuple)) else [xs]
    T, K = sum(x.shape[0] for x in xs), xs[0].shape[1]
    N = w.shape[1]
    tm, tn = _tile(min(x.shape[0] for x in xs), tm), _tile(N, tn)
    bounds, index_map = _row_parts(xs, tm)
    return pl.pallas_call(
        functools.partial(_norm_matmul_kernel, bounds=bounds),
        grid=(T // tm, N // tn),
        in_specs=[pl.BlockSpec((tm, K), index_map(p, lambda j: 0)) for p in range(len(xs))] + [
                  pl.BlockSpec((1, K), lambda i, j: (0, 0)),
                  pl.BlockSpec((K, tn), lambda i, j: (0, j))],
        out_specs=pl.BlockSpec((tm, tn), lambda i, j: (i, j)),
        out_shape=jax.ShapeDtypeStruct((T, N), out_dtype),
        scratch_shapes=[pltpu.VMEM((tm, K), BF16)],
        compiler_params=_params(("parallel", "arbitrary")),
        name="norm_matmul",
    )(*xs, g.reshape(1, K).astype(F32), w)


def _matmul_residual_kernel(a_ref, w_ref, *refs, bounds):
    x_refs, o_ref = refs[:len(bounds)], refs[len(bounds)]
    o_ref[...] = _read_part(bounds, x_refs) + _dot(a_ref[...], w_ref[...])


def _matmul_residual(a, w, xs, tm=1024, tn=512):
    xs = xs if isinstance(xs, (list, tuple)) else [xs]
    T, K = a.shape
    N = w.shape[1]
    tm, tn = _tile(min(x.shape[0] for x in xs), tm), _tile(N, tn)
    bounds, index_map = _row_parts(xs, tm)
    return pl.pallas_call(
        functools.partial(_matmul_residual_kernel, bounds=bounds),
        grid=(T // tm, N // tn),
        in_specs=[pl.BlockSpec((tm, K), lambda i, j: (i, 0)),
                  pl.BlockSpec((K, tn), lambda i, j: (0, j))] + [
                  pl.BlockSpec((tm, tn), index_map(p, lambda j: j)) for p in range(len(xs))],
        out_specs=pl.BlockSpec((tm, tn), lambda i, j: (i, j)),
        out_shape=jax.ShapeDtypeStruct((T, N), F32),
        input_output_aliases={2: 0} if len(xs) == 1 else {},
        compiler_params=_params(("parallel", "arbitrary")),
        name="matmul_residual",
    )(a, w, *xs)


def _mlp_kernel(x_ref, g_ref, w1_ref, w2_ref, o_ref, xn_ref):
    @pl.when(pl.program_id(1) == 0)
    def _():
        x = x_ref[...]
        xn_ref[...] = (_rms_scale(x) * g_ref[...]).astype(BF16)
        o_ref[...] = x

    tf = w1_ref.shape[1]
    chunk = min(MLP_FF_CHUNK, tf)
    for c in range(tf // chunk):
        cs = slice(c * chunk, (c + 1) * chunk)
        a = jnp.maximum(_dot(xn_ref[...], w1_ref[:, cs]), 0.0)
        o_ref[...] += _dot((a * a).astype(BF16), w2_ref[cs, :])


def _mlp(x, g, w1, w2, layer, row0=0, rows=None, tm=512, tf=2048):
    D = x.shape[1]
    rows = x.shape[0] if rows is None else rows
    Fd = w1.shape[2]
    tm, tf = _tile(rows, tm), _tile(Fd, tf)
    assert row0 % tm == 0
    blk0 = row0 // tm
    return pl.pallas_call(
        _mlp_kernel,
        grid=(rows // tm, Fd // tf),
        in_specs=[pl.BlockSpec((tm, D), lambda i, f: (blk0 + i, 0)),
                  pl.BlockSpec((1, D), lambda i, f: (0, 0)),
                  pl.BlockSpec((None, D, tf), lambda i, f: (layer, 0, f)),
                  pl.BlockSpec((None, tf, D), lambda i, f: (layer, f, 0))],
        out_specs=pl.BlockSpec((tm, D), lambda i, f: (i, 0)),
        out_shape=jax.ShapeDtypeStruct((rows, D), F32),
        scratch_shapes=[pltpu.VMEM((tm, D), BF16)],
        compiler_params=_params(("parallel", "arbitrary")),
        name="mlp",
    )(x, g.reshape(1, D).astype(F32), w1, w2)


def _xattn_kernel(x_ref, g_ref, wq_ref, kv_ref, gq_ref, gk_ref, wo_ref, o_ref):
    x = x_ref[0]
    xn = (_rms_scale(x) * g_ref[...]).astype(BF16)
    q = _dot(xn, wq_ref[...])
    kv = kv_ref[0].astype(F32)
    width = MEM_HEADS * MEM_HD
    outs = []
    for h in range(MEM_HEADS):
        cs = slice(h * MEM_HD, (h + 1) * MEM_HD)
        qh = (_rms_scale(q[:, cs]) * gq_ref[...]).astype(BF16)
        kh = (_rms_scale(kv[:, cs]) * gk_ref[...]).astype(BF16)
        vh = kv_ref[0, :, width + h * MEM_HD: width + (h + 1) * MEM_HD]
        s = _dot_nt(qh, kh)
        p = jnp.exp(s - jnp.max(s, axis=-1, keepdims=True))
        l = jnp.sum(p, axis=-1, keepdims=True)
        outs.append((_dot(p.astype(BF16), vh) / l).astype(BF16))
    o = jnp.concatenate(outs, axis=-1)
    o_ref[0] = x + _dot(o, wo_ref[...])


def _xattn(x, kv, g, wq, gq, gk, wo, tm=1024):
    B, L, D = x.shape
    M = kv.shape[1]
    width = MEM_HEADS * MEM_HD
    tm = _tile(L, tm)
    return pl.pallas_call(
        _xattn_kernel,
        grid=(B, L // tm),
        in_specs=[pl.BlockSpec((1, tm, D), lambda b, i: (b, i, 0)),
                  pl.BlockSpec((1, D), lambda b, i: (0, 0)),
                  pl.BlockSpec((D, width), lambda b, i: (0, 0)),
                  pl.BlockSpec((1, M, 2 * width), lambda b, i: (b, 0, 0)),
                  pl.BlockSpec((1, MEM_HD), lambda b, i: (0, 0)),
                  pl.BlockSpec((1, MEM_HD), lambda b, i: (0, 0)),
                  pl.BlockSpec((width, D), lambda b, i: (0, 0))],
        out_specs=pl.BlockSpec((1, tm, D), lambda b, i: (b, i, 0)),
        out_shape=jax.ShapeDtypeStruct((B, L, D), F32),
        input_output_aliases={0: 0},
        compiler_params=_params(("parallel", "arbitrary")),
        name="mem_xattn",
    )(x, g.reshape(1, D).astype(F32), wq, kv,
      (gq.astype(F32) * (MEM_HD ** -0.5)).reshape(1, MEM_HD), gk.reshape(1, MEM_HD).astype(F32), wo)


def _attn_kernel(q_ref, k_ref, v_ref, o_ref, *, heads, q_per_kv, dq, dv):
    def scores(g):
        kg = g // q_per_kv
        return _dot_nt(q_ref[0, :, g * dq:(g + 1) * dq], k_ref[0, :, kg * dq:(kg + 1) * dq])

    s_next = scores(0)
    for g in range(heads):
        s, s_next = s_next, (scores(g + 1) if g + 1 < heads else None)
        kg = g // q_per_kv
        v = v_ref[0, :, kg * dv:(kg + 1) * dv]
        p = jnp.exp2(s - jnp.max(s, axis=-1, keepdims=True))
        l = jnp.sum(p, axis=-1, keepdims=True)
        o_ref[0, :, g * dv:(g + 1) * dv] = (_dot(p.astype(BF16), v) / l).astype(o_ref.dtype)


def _attention(q, k, v, *, n_heads, q_per_kv, dq, dv, k_col0, v_col0, tq=512):
    B, L = q.shape[0], q.shape[1]
    tq = _tile(L, tq)
    heads = ATTN_HEADS_PER_STEP
    kvh = heads // q_per_kv
    k_block0, v_block0 = k_col0 // (kvh * dq), v_col0 // (kvh * dv)
    assert n_heads % heads == 0 and k_col0 % (kvh * dq) == 0 and v_col0 % (kvh * dv) == 0
    return pl.pallas_call(
        functools.partial(_attn_kernel, heads=heads, q_per_kv=q_per_kv, dq=dq, dv=dv),
        grid=(B, n_heads // heads, L // tq),
        in_specs=[pl.BlockSpec((1, tq, heads * dq), lambda b, h, i: (b, i, h)),
                  pl.BlockSpec((1, L, kvh * dq), lambda b, h, i: (b, 0, k_block0 + h)),
                  pl.BlockSpec((1, L, kvh * dv), lambda b, h, i: (b, 0, v_block0 + h))],
        out_specs=pl.BlockSpec((1, tq, heads * dv), lambda b, h, i: (b, i, h)),
        out_shape=jax.ShapeDtypeStruct((B, L, n_heads * dv), BF16),
        compiler_params=_params(("parallel", "parallel", "arbitrary")),
        name="softmax_attention",
    )(q, k, v)


def _retention_kernel(q_ref, k_ref, v_ref, g_ref, cos_ref, sin_ref, dec_ref, on_ref, o_ref, *, tq):
    for h in range(RET_HEADS_PER_STEP):
        qk = pl.ds(h * RET_DK, RET_DK)
        vg = pl.ds(h * RET_DV, RET_DV)
        _retention_head(q_ref.at[:, :, qk], k_ref.at[:, :, qk], v_ref.at[:, :, vg], g_ref.at[:, :, vg],
                        cos_ref, sin_ref, dec_ref.at[pl.ds(h, 1)], on_ref, o_ref.at[:, :, vg], tq=tq)


def _retention_head(q_ref, k_ref, v_ref, g_ref, cos_ref, sin_ref, dec_ref, on_ref, o_ref, *, tq):
    L = q_ref.shape[1]
    half = RET_DK // 2
    cos = cos_ref[...]
    sin = sin_ref[...]

    def rope(ref, scale):
        x1 = ref[0, :, :half].astype(F32)
        x2 = ref[0, :, half:].astype(F32)
        y = jnp.concatenate([x1 * cos - x2 * sin, x1 * sin + x2 * cos], axis=-1)
        return (y * scale).astype(BF16)

    q = rope(q_ref, RET_DK ** -0.5)
    k = rope(k_ref, 1.0)
    v = v_ref[0]
    dec = dec_ref[0]
    lg = jnp.minimum(dec, 0.0) - jnp.log1p(jnp.exp(-jnp.abs(dec)))
    lg_f = lg[0:1, 0:1]
    lg_b = lg[1:2, 0:1]
    n = L // tq
    ri = lax.broadcasted_iota(jnp.int32, (tq, tq), 0)
    diff = (ri - lax.broadcasted_iota(jnp.int32, (tq, tq), 1)).astype(F32)
    d_intra = jnp.where(diff == 0, 2.0, jnp.exp(jnp.where(diff >= 0, diff * lg_f, -diff * lg_b)))
    r = lax.broadcasted_iota(jnp.int32, (tq, RET_DK), 0).astype(F32)
    qdec = (jnp.exp((r + 1.0) * lg_f), jnp.exp((tq - r) * lg_b))
    kdec = (jnp.exp((tq - 1.0 - r) * lg_f), jnp.exp(r * lg_b))
    gc = (jnp.exp(tq * lg_f), jnp.exp(tq * lg_b))
    rows = [slice(i * tq, (i + 1) * tq) for i in range(n)]

    def kv_state(m, d):
        kd = (k[rows[m]].astype(F32) * kdec[d]).T.astype(BF16)
        return _dot(kd, v[rows[m]])

    state = [[None] * n, [None] * n]
    for i in range(1, n):
        prev = state[0][i - 1]
        state[0][i] = kv_state(i - 1, 0) if prev is None else prev * gc[0] + kv_state(i - 1, 0)
    for i in range(n - 2, -1, -1):
        nxt = state[1][i + 1]
        state[1][i] = kv_state(i + 1, 1) if nxt is None else nxt * gc[1] + kv_state(i + 1, 1)
    for i in range(n):
        qi = q[rows[i]]
        s = _dot_nt(qi, k[rows[i]]) * d_intra
        o = _dot(s.astype(BF16), v[rows[i]])
        for d in (0, 1):
            if state[d][i] is not None:
                o = o + _dot((qi.astype(F32) * qdec[d]).astype(BF16), state[d][i].astype(BF16))
        y = _rms_scale(o) * on_ref[...]
        gate = g_ref[0, rows[i], :].astype(F32)
        o_ref[0, rows[i], :] = (y * (gate * _sigmoid(gate))).astype(o_ref.dtype)


def _retention(p, cos, sin, dec, out_norm, tq=512):
    B, L, _ = p.shape
    H = RET_HEADS
    hp = RET_HEADS_PER_STEP
    tq = _tile(L, tq)
    n = H // hp
    return pl.pallas_call(
        functools.partial(_retention_kernel, tq=tq),
        grid=(n, B),
        in_specs=[pl.BlockSpec((1, L, hp * RET_DK), lambda h, b: (b, 0, h)),
                  pl.BlockSpec((1, L, hp * RET_DK), lambda h, b: (b, 0, n + h)),
                  pl.BlockSpec((1, L, hp * RET_DV), lambda h, b: (b, 0, n + h)),
                  pl.BlockSpec((1, L, hp * RET_DV), lambda h, b: (b, 0, 2 * n + h)),
                  pl.BlockSpec((L, RET_DK // 2), lambda h, b: (0, 0)),
                  pl.BlockSpec((L, RET_DK // 2), lambda h, b: (0, 0)),
                  pl.BlockSpec((hp, 2, LANES), lambda h, b: (h, 0, 0)),
                  pl.BlockSpec((1, RET_DV), lambda h, b: (0, 0))],
        out_specs=pl.BlockSpec((1, L, hp * RET_DV), lambda h, b: (b, 0, h)),
        out_shape=jax.ShapeDtypeStruct((B, L, H * RET_DV), BF16),
        compiler_params=_params(("parallel", "arbitrary")),
        name="retention",
    )(p, p, p, p, cos, sin, dec, out_norm.reshape(1, RET_DV).astype(F32))


def _hgrn_kernel(q_ref, ff_ref, fb_ref, i_ref, g_ref, lb_ref, on_ref, m_ref, o_ref,
                 of_ref, ob_ref, st_ref, qt_ref, kt_ref, ks_ref, dec_ref, *, heads):
    L = q_ref.shape[1]
    nblk = L // HG_BLOCK
    n_chunks = HG_BLOCK // HG_CHUNK
    width = heads * HG_DK
    rows = lax.broadcasted_iota(jnp.int32, (HG_BLOCK, HG_BLOCK), 0)
    cols = lax.broadcasted_iota(jnp.int32, (HG_BLOCK, HG_BLOCK), 1)
    same_chunk = (rows // HG_CHUNK) == (cols // HG_CHUNK)
    masks = (same_chunk & (cols <= rows), same_chunk & (cols >= rows))
    row_chunk = lax.broadcasted_iota(jnp.int32, (HG_BLOCK, HG_DK), 0) // HG_CHUNK
    chunk_rows = [jnp.where(row_chunk == c, 1.0, 0.0).astype(BF16) for c in range(n_chunks)]
    st_ref[...] = jnp.zeros_like(st_ref)

    def block_rows(t, d):
        r = t if d == 0 else nblk - 1 - t
        return pl.ds(pl.multiple_of(r * HG_BLOCK, HG_BLOCK), HG_BLOCK)

    def prep(t, d, slot):
        rs = block_rows(t, d)
        fz = (ff_ref if d == 0 else fb_ref)[0, rs, :].astype(F32)
        lb = lb_ref[...]
        f = lb + (1.0 - lb) * _sigmoid(fz)
        lf = jnp.log(f)
        kk = 1.0 - f
        hi = lf.astype(BF16)
        lo = (lf - hi.astype(F32)).astype(BF16)
        cb = _dot(m_ref[d], jnp.concatenate([hi, lo], axis=1))
        b = cb[:, :width] + cb[:, width:]
        last = [c * HG_CHUNK + (HG_CHUNK - 1 if d == 0 else 0) for c in range(n_chunks)]
        dec_rows = [jnp.exp(b[i:i + 1, :]) for i in last]
        dec = jnp.concatenate([jnp.broadcast_to(row, (HG_CHUNK, width)) for row in dec_rows], axis=0)
        kt = kk * jnp.exp(-b)
        qt_ref[slot, d] = (q_ref[0, rs, :].astype(F32) * jnp.exp(b)).astype(BF16)
        kt_ref[slot, d] = kt.astype(BF16)
        ks_ref[slot, d] = (kt * dec).astype(BF16)
        for c in range(n_chunks):
            dec_ref[slot, d, c:c + 1, :] = dec_rows[c]

    items = [(h, d) for h in range(heads) for d in (0, 1)]

    def half(t, slot):
        rs = [block_rows(t, d) for d in (0, 1)]
        v = {(h, d): i_ref[0, rs[d], h * HG_DK:(h + 1) * HG_DK] for h, d in items}
        qt = {(h, d): qt_ref[slot, d, :, h * HG_DK:(h + 1) * HG_DK] for h, d in items}
        s = {(h, d): _dot_nt(qt[h, d], kt_ref[slot, d, :, h * HG_DK:(h + 1) * HG_DK]) for h, d in items}
        kv = {}
        for h, d in items:
            ks = ks_ref[slot, d, :, h * HG_DK:(h + 1) * HG_DK]
            ks_cols = jnp.concatenate([ks * chunk_rows[c] for c in range(n_chunks)], axis=1)
            kv[h, d] = _dot(v[h, d].astype(F32).T.astype(BF16), ks_cols)
        t_next = jnp.minimum(t + 1, nblk - 1)
        prep(t_next, 0, 1 - slot)
        s = {it: jnp.where(masks[it[1]], s[it], 0.0).astype(BF16) for it in items}
        o = {it: _dot(s[it], v[it]) for it in items}
        st = {(h, d): st_ref[2 * h + d] for h, d in items}
        parts = {it: [None] * n_chunks for it in items}
        for step in range(n_chunks):
            if step == n_chunks // 2:
                prep(t_next, 1, 1 - slot)
            for h, d in items:
                c = step if d == 0 else n_chunks - 1 - step
                cr = slice(c * HG_CHUNK, (c + 1) * HG_CHUNK)
                parts[h, d][c] = o[h, d][cr] + _dot_nt(qt[h, d][cr], st[h, d].astype(BF16))
                dec = dec_ref[slot, d, c:c + 1, h * HG_DK:(h + 1) * HG_DK]
                st[h, d] = st[h, d] * dec + kv[h, d][:, c * HG_DK:(c + 1) * HG_DK]
        for h, d in items:
            st_ref[2 * h + d] = st[h, d]
        for d in (0, 1):
            out = jnp.concatenate([jnp.concatenate(parts[h, d], axis=0) for h in range(heads)], axis=1)
            (of_ref if d == 0 else ob_ref)[rs[d], :] = out

    def body(u, carry):
        half(2 * u, 0)
        half(2 * u + 1, 1)
        return carry

    for d in (0, 1):
        prep(jnp.int32(0), d, 0)
    lax.fori_loop(0, nblk // 2, body, 0)
    for h in range(heads):
        cs = slice(h * HG_DV, (h + 1) * HG_DV)
        y = _rms_scale(of_ref[:, cs] + ob_ref[:, cs]) * on_ref[...]
        gate = g_ref[0, :, cs].astype(F32)
        o_ref[0, :, cs] = (y * (gate * _sigmoid(gate))).astype(o_ref.dtype)


def _hgrn(p, lb, out_norm):
    B, L, _ = p.shape
    heads = HG_HEADS_PER_STEP
    nb = HG_HEADS // heads
    w = heads * HG_DK
    c = HG_CHUNK
    n_chunks = HG_BLOCK // HG_CHUNK
    assert (L // HG_BLOCK) % 2 == 0
    idx = jnp.arange(HG_BLOCK)
    same = (idx[:, None] // c) == (idx[None, :] // c)
    tri_f = same & (idx[None, :] <= idx[:, None])
    tri_b = same & (idx[None, :] >= idx[:, None])
    m = jnp.stack([tri_f, tri_b]).astype(BF16)
    spec = lambda off: pl.BlockSpec((1, L, w), lambda b, j: (b, 0, off * nb + j))
    return pl.pallas_call(
        functools.partial(_hgrn_kernel, heads=heads),
        grid=(B, nb),
        in_specs=[spec(0), spec(1), spec(2), spec(3), spec(4),
                  pl.BlockSpec((1, w), lambda b, j: (0, j)),
                  pl.BlockSpec((1, HG_DV), lambda b, j: (0, 0)),
                  pl.BlockSpec((2, HG_BLOCK, HG_BLOCK), lambda b, j: (0, 0, 0))],
        out_specs=pl.BlockSpec((1, L, w), lambda b, j: (b, 0, j)),
        out_shape=jax.ShapeDtypeStruct((B, L, HG_HEADS * HG_DV), BF16),
        scratch_shapes=[pltpu.VMEM((L, w), F32), pltpu.VMEM((L, w), F32),
                        pltpu.VMEM((2 * heads, HG_DV, HG_DK), F32),
                        pltpu.VMEM((2, 2, HG_BLOCK, w), BF16), pltpu.VMEM((2, 2, HG_BLOCK, w), BF16),
                        pltpu.VMEM((2, 2, HG_BLOCK, w), BF16), pltpu.VMEM((2, 2, n_chunks, w), F32)],
        compiler_params=_params(("parallel", "arbitrary")),
        name="hgrn2",
    )(p, p, p, p, p, lb.reshape(1, -1).astype(F32), out_norm.reshape(1, HG_DV).astype(F32), m)


def _mla_q_kernel(c_ref, gn_ref, w_ref, gh_ref, cos_ref, sin_ref, o_ref, xn_ref):
    @pl.when(pl.program_id(1) == 0)
    def _():
        xn_ref[...] = (_rms_scale(c_ref[...]) * gn_ref[...]).astype(BF16)

    hw = 2 * LANES
    head_dot = lambda g: _dot(xn_ref[...], w_ref[:, g * hw:(g + 1) * hw])
    y_next = head_dot(0)
    for g in range(MLA_HEADS_PER_STEP):
        y, y_next = y_next, (head_dot(g + 1) if g + 1 < MLA_HEADS_PER_STEP else None)
        inv = _inv_rms_lanes(y, MLA_NOPE + MLA_ROPE)
        gh = gh_ref[...]
        o_ref[:, g * hw:g * hw + MLA_NOPE] = (y[:, :MLA_NOPE] * inv * gh[:, :MLA_NOPE]).astype(o_ref.dtype)
        o_ref[:, g * hw + MLA_NOPE:(g + 1) * hw] = _rope_pairs(
            y[:, MLA_NOPE:] * inv * gh[:, MLA_NOPE:], cos_ref[...], sin_ref[...]).astype(o_ref.dtype)


def _mla_kv_kernel(c_ref, kr_ref, gn_ref, w_ref, gh_ref, cos_ref, sin_ref, k_ref, v_ref, xn_ref):
    @pl.when(pl.program_id(1) == 0)
    def _():
        xn_ref[...] = (_rms_scale(c_ref[...]) * gn_ref[...]).astype(BF16)

    hw = 2 * LANES
    kr = kr_ref[...]
    gh = gh_ref[...]
    head_dot = lambda g: _dot(xn_ref[...], w_ref[:, g * hw:(g + 1) * hw])
    y_next = head_dot(0)
    for g in range(MLA_HEADS_PER_STEP):
        y, y_next = y_next, (head_dot(g + 1) if g + 1 < MLA_HEADS_PER_STEP else None)
        kn = y[:, :MLA_NOPE]
        inv = _inv_rms_lanes(jnp.concatenate([kn, kr], axis=1), MLA_NOPE + MLA_ROPE)
        k_ref[:, g * hw:g * hw + MLA_NOPE] = (kn * inv * gh[:, :MLA_NOPE]).astype(k_ref.dtype)
        k_ref[:, g * hw + MLA_NOPE:(g + 1) * hw] = _rope_pairs(
            kr * inv * gh[:, MLA_NOPE:], cos_ref[...], sin_ref[...]).astype(k_ref.dtype)
        v_ref[:, g * MLA_V:(g + 1) * MLA_V] = y[:, MLA_NOPE:].astype(v_ref.dtype)


def _mla_qkv(c, L, q_norm, kv_norm, wq, wkv, gq, gk, tabs, tm=1024):
    T = c.shape[0]
    H = MLA_HEADS
    hps = MLA_HEADS_PER_STEP
    tm = _tile(L, tm)
    lt = L // tm
    hw = 2 * LANES
    tab_spec = pl.BlockSpec((tm, LANES), lambda i, h: (i % lt, 0))
    row = lambda n: pl.BlockSpec((1, n), lambda i, h: (0, 0))
    q = pl.pallas_call(
        _mla_q_kernel,
        grid=(T // tm, H // hps),
        in_specs=[pl.BlockSpec((tm, MLA_Q_RANK), lambda i, h: (i, 0)), row(MLA_Q_RANK),
                  pl.BlockSpec((MLA_Q_RANK, hps * hw), lambda i, h: (0, h)), row(hw),
                  tab_spec, tab_spec],
        out_specs=pl.BlockSpec((tm, hps * hw), lambda i, h: (i, h)),
        out_shape=jax.ShapeDtypeStruct((T, H * hw), BF16),
        scratch_shapes=[pltpu.VMEM((tm, MLA_Q_RANK), BF16)],
        compiler_params=_params(("parallel", "arbitrary")),
        name="mla_q",
    )(c, q_norm.reshape(1, -1).astype(F32), wq, gq, *tabs)
    k, v = pl.pallas_call(
        _mla_kv_kernel,
        grid=(T // tm, H // hps),
        in_specs=[pl.BlockSpec((tm, MLA_KV_RANK), lambda i, h: (i, 1)),
                  pl.BlockSpec((tm, LANES), lambda i, h: (i, (MLA_Q_RANK + MLA_KV_RANK) // LANES)),
                  row(MLA_KV_RANK),
                  pl.BlockSpec((MLA_KV_RANK, hps * hw), lambda i, h: (0, h)), row(hw),
                  tab_spec, tab_spec],
        out_specs=[pl.BlockSpec((tm, hps * hw), lambda i, h: (i, h)),
                   pl.BlockSpec((tm, hps * MLA_V), lambda i, h: (i, h))],
        out_shape=[jax.ShapeDtypeStruct((T, H * hw), BF16), jax.ShapeDtypeStruct((T, H * MLA_V), BF16)],
        scratch_shapes=[pltpu.VMEM((tm, MLA_KV_RANK), BF16)],
        compiler_params=_params(("parallel", "arbitrary")),
        name="mla_kv",
    )(c, c, kv_norm.reshape(1, -1).astype(F32), wkv, gk, *tabs)
    return q, k, v


def _head_norm_rope_kernel(x_ref, g_ref, cos_ref, sin_ref, o_ref, *, n_heads):
    for h in range(n_heads):
        cs = slice(h * LANES, (h + 1) * LANES)
        x = x_ref[:, cs].astype(F32)
        y = x * _inv_rms_lanes(x, LANES) * g_ref[:, cs]
        o_ref[:, cs] = _rope_pairs(y, cos_ref[...], sin_ref[...]).astype(o_ref.dtype)


def _head_norm_rope(p, L, n_heads, gains, tabs, tm=512):
    T = p.shape[0]
    tm = _tile(L, tm)
    lt = L // tm
    w = n_heads * LANES
    tab_spec = pl.BlockSpec((tm, LANES), lambda i: (i % lt, 0))
    return pl.pallas_call(
        functools.partial(_head_norm_rope_kernel, n_heads=n_heads),
        grid=(T // tm,),
        in_specs=[pl.BlockSpec((tm, w), lambda i: (i, 0)),
                  pl.BlockSpec((1, w), lambda i: (0, 0)),
                  tab_spec, tab_spec],
        out_specs=pl.BlockSpec((tm, w), lambda i: (i, 0)),
        out_shape=jax.ShapeDtypeStruct((T, w), BF16),
        compiler_params=_params(("parallel",)),
        name="head_norm_rope",
    )(p, gains, *tabs)


ROPE_HALF = 32


def _rope_tables(pos_a, pos_b, base):
    freqs = base ** (-jnp.arange(ROPE_HALF, dtype=F32) / ROPE_HALF)

    def cs(pos):
        ang = pos[:, None] * freqs[None, :]
        return jnp.cos(ang), jnp.sin(ang)

    ca, sa = cs(pos_a)
    cb, sb = (jnp.zeros_like(ca), jnp.zeros_like(sa)) if pos_b is None else cs(pos_b)
    return jnp.concatenate([ca, cb, ca, cb], axis=-1), jnp.concatenate([-sa, -sb, sa, sb], axis=-1)


def _spread_pairs(a):
    z = jnp.zeros(a.shape[:-1] + (ROPE_HALF,), a.dtype)
    return jnp.concatenate([a[..., :ROPE_HALF], z, a[..., ROPE_HALF:], z], axis=-1)


def _interleave_halves(a):
    q = [a[..., i * ROPE_HALF:(i + 1) * ROPE_HALF] for i in range(4)]
    return jnp.concatenate([q[0], q[2], q[1], q[3]], axis=-1)


def _trunk(xs, mem, w):
    group_batches = [x.shape[0] for x in xs]
    B, (L, D) = sum(group_batches), xs[0].shape[1:]
    T = B * L
    M = mem.shape[1]
    x = [x.reshape(-1, D) for x in xs]
    mem = mem.reshape(B * M, D)
    pos = jnp.arange(L, dtype=F32)
    sm = jax.nn.softmax(w['hg_lb'].astype(F32), axis=0)
    lb_all = jnp.cumsum(sm, axis=0) - sm[0]
    depth = w['norm_mix'].shape[0]
    for i in range(depth):
        kind, j = i % 4, i // 4
        if kind == 0:
            p = _norm_matmul(x, w['norm_mix'][i], w['ret_w_in'][j], BF16)
            half = RET_DK // 2
            ang = pos[:, None] * (RET_ROPE_BASE ** (-jnp.arange(half, dtype=F32) / half))[None, :]
            dec = jnp.broadcast_to(w['ret_decay'][j].astype(F32).T[:, :, None], (RET_HEADS, 2, LANES))
            o = _retention(p.reshape(B, L, -1), jnp.cos(ang), jnp.sin(ang), dec, w['ret_out_norm'][j])
            x = _matmul_residual(o.reshape(T, -1), w['ret_w_out'][j], x)
        elif kind == 1:
            p = _norm_matmul(x, w['norm_mix'][i], w['hg_w_in'][j], BF16, tn=2048)
            o = _hgrn(p.reshape(B, L, -1), lb_all[i], w['hg_out_norm'][j])
            x = _matmul_residual(o.reshape(T, -1), w['hg_w_out'][j], x, tm=512, tn=D)
        elif kind == 2:
            c = _norm_matmul(x, w['norm_mix'][i], w['mla_w_in'][j], F32)
            tabs = _rope_tables(pos, None, MLA_ROPE_BASE)
            q, k, v = _mla_qkv(c, L, w['mla_q_norm'][j], w['mla_kv_norm'][j], w['mla_w_qb'][j], w['mla_w_kvb'][j],
                               w['mla_gq'][j], w['mla_gk'][j], tabs)
            o = _attention(q.reshape(B, L, -1), k.reshape(B, L, -1), v.reshape(B, L, -1),
                           n_heads=MLA_HEADS, q_per_kv=1, dq=2 * LANES, dv=MLA_V, k_col0=0, v_col0=0)
            x = _matmul_residual(o.reshape(T, -1), w['mla_w_out'][j], x, tm=512, tn=D)
        else:
            p = _norm_matmul(x, w['norm_mix'][i], w['gqa_w_in'][j], BF16)
            t = jnp.arange(L)
            tabs = _rope_tables((t // GRID_W).astype(F32), (t % GRID_W).astype(F32), GQA_ROPE_BASE)
            nqk = GQA_HEADS + GQA_KV_HEADS
            qk = _head_norm_rope(p, L, nqk, w['gqa_gains'][j], tabs)
            o = _attention(qk.reshape(B, L, -1), qk.reshape(B, L, -1), p.reshape(B, L, -1),
                           n_heads=GQA_HEADS, q_per_kv=GQA_HEADS // GQA_KV_HEADS, dq=GQA_HD, dv=GQA_HD,
                           k_col0=GQA_HEADS * GQA_HD, v_col0=nqk * GQA_HD)
            x = _matmul_residual(o.reshape(T, -1), w['gqa_w_out'][j], x, tm=512, tn=D)
        kv = _norm_matmul(mem, w['norm_memtok'][i], w['mem_w_kv'][i], BF16)
        x = _xattn(x.reshape(B, L, D), kv.reshape(B, M, -1), w['norm_mem'][i], w['mem_w_q'][i],
                   w['mem_qk_norm'][i, 0], w['mem_qk_norm'][i, 1], w['mem_w_out'][i]).reshape(T, D)
        mlp_w = (w['norm_mlp'][i], w['mlp_w1'], w['mlp_w2'], i)
        if i < depth - 1:
            x = _mlp(x, *mlp_w)
    outs = []
    row0 = 0
    for nb in group_batches:
        outs.append(_mlp(x, *mlp_w, row0=row0, rows=nb * L).reshape(nb, L, D))
        row0 += nb * L
    return tuple(outs)


def _prepare_weights(w):
    out = dict(w)
    for name in ('ret_w_in', 'ret_w_out', 'hg_w_in', 'hg_w_out', 'mla_w_out', 'gqa_w_in', 'gqa_w_out',
                 'mem_w_q', 'mem_w_kv', 'mem_w_out', 'mlp_w1', 'mlp_w2', 'mla_w_kvb'):
        out[name] = w[name].astype(BF16)
    n = w['mla_w_in'].shape[0]
    lat = MLA_Q_RANK + MLA_KV_RANK
    out['mla_w_in'] = jnp.concatenate([w['mla_w_in'][..., :lat], _spread_pairs(w['mla_w_in'][..., lat:])],
                                      axis=-1).astype(BF16)
    wq = w['mla_w_qb'].reshape(n, MLA_Q_RANK, MLA_HEADS, MLA_NOPE + MLA_ROPE)
    wq = jnp.concatenate([wq[..., :MLA_NOPE], _spread_pairs(wq[..., MLA_NOPE:])], axis=-1)
    out['mla_w_qb'] = wq.reshape(n, MLA_Q_RANK, -1).astype(BF16)
    g = w['mla_qk_norm'].astype(F32)
    g = jnp.concatenate([g[..., :MLA_NOPE], _spread_pairs(g[..., MLA_NOPE:])], axis=-1)
    out['mla_gq'] = g[:, 0:1] * ((MLA_NOPE + MLA_ROPE) ** -0.5 * LOG2_E)
    out['mla_gk'] = g[:, 1:2]
    nqk = GQA_HEADS + GQA_KV_HEADS
    wi = w['gqa_w_in']
    wqk = _interleave_halves(wi[..., :nqk * GQA_HD].reshape(wi.shape[0], wi.shape[1], nqk, GQA_HD))
    out['gqa_w_in'] = jnp.concatenate([wqk.reshape(wi.shape[0], wi.shape[1], -1), wi[..., nqk * GQA_HD:]],
                                      axis=-1).astype(BF16)
    g = _interleave_halves(w['gqa_qk_norm'].astype(F32))
    out['gqa_gains'] = jnp.concatenate([jnp.tile(g[:, 0] * (GQA_HD ** -0.5 * LOG2_E), (1, GQA_HEADS)),
                                        jnp.tile(g[:, 1], (1, GQA_KV_HEADS))], axis=-1)[:, None, :]
    return out


def kernel(x_prompt, x_sample, mem_prompt, mem_sample, norm_mix, norm_mem, norm_memtok, norm_mlp, ret_w_in, ret_decay, ret_out_norm, ret_w_out, hg_w_in, hg_lb, hg_out_norm, hg_w_out, mla_w_in, mla_q_norm, mla_kv_norm, mla_w_qb, mla_w_kvb, mla_qk_norm, mla_w_out, gqa_w_in, gqa_qk_norm, gqa_w_out, mem_w_q, mem_w_kv, mem_qk_norm, mem_w_out, mlp_w1, mlp_w2):
    w = _prepare_weights(dict(
        norm_mix=norm_mix, norm_mem=norm_mem, norm_memtok=norm_memtok, norm_mlp=norm_mlp,
        ret_w_in=ret_w_in, ret_decay=ret_decay, ret_out_norm=ret_out_norm, ret_w_out=ret_w_out,
        hg_w_in=hg_w_in, hg_lb=hg_lb, hg_out_norm=hg_out_norm, hg_w_out=hg_w_out,
        mla_w_in=mla_w_in, mla_q_norm=mla_q_norm, mla_kv_norm=mla_kv_norm, mla_w_qb=mla_w_qb,
        mla_w_kvb=mla_w_kvb, mla_qk_norm=mla_qk_norm, mla_w_out=mla_w_out,
        gqa_w_in=gqa_w_in, gqa_qk_norm=gqa_qk_norm, gqa_w_out=gqa_w_out,
        mem_w_q=mem_w_q, mem_w_kv=mem_w_kv, mem_qk_norm=mem_qk_norm, mem_w_out=mem_w_out,
        mlp_w1=mlp_w1, mlp_w2=mlp_w2))
    mem = jnp.concatenate([mem_prompt, mem_sample], axis=0)
    return _trunk([x_prompt, x_sample], mem, w)
```

```python
import functools

import jax
import jax.numpy as jnp
from jax import lax
from jax.experimental import pallas as pl
from jax.experimental.pallas import tpu as pltpu

F32 = jnp.float32
BF16 = jnp.bfloat16

D_MODEL = 2048
GRID_W = 64
NORM_EPS = 1e-6
LOG2_E = 1.4426950408889634
RET_HEADS, RET_DK, RET_DV = 8, 256, 512
RET_ROPE_BASE = 10000.0
HG_HEADS, HG_DK, HG_DV, HG_CHUNK = 16, 128, 128, 32
MLA_HEADS, MLA_Q_RANK, MLA_KV_RANK, MLA_NOPE, MLA_ROPE, MLA_V = 16, 512, 512, 128, 64, 128
MLA_ROPE_BASE = 10000.0
GQA_HEADS, GQA_KV_HEADS, GQA_HD = 16, 4, 128
GQA_ROPE_BASE = 10000.0
MEM_HEADS, MEM_HD = 4, 128

LANES = 128
VMEM_LIMIT_BYTES = 56 * 2 ** 20

HG_BLOCK = 128
HG_HEADS_PER_STEP = 4
PREP_PARTS = 2
MLA_HEADS_PER_STEP = 8
ATTN_HEADS_PER_STEP = 8
RET_HEADS_PER_STEP = 1
MLP_FF_CHUNK = 1024


def _params(sem):
    return pltpu.CompilerParams(dimension_semantics=sem, vmem_limit_bytes=VMEM_LIMIT_BYTES)


def _tile(n, pref):
    return pref if n % pref == 0 else n


def _rms_scale(x, width=None):
    width = x.shape[-1] if width is None else width
    ms = jnp.sum(x * x, axis=-1, keepdims=True) * (1.0 / width)
    return x * lax.rsqrt(ms + NORM_EPS)


def _inv_rms_lanes(x, width):
    sq = (x * x).astype(BF16)
    ss = _dot(sq, jnp.ones((x.shape[-1], LANES), BF16))
    return lax.rsqrt(ss * (1.0 / width) + NORM_EPS)


def _sigmoid(x):
    return 1.0 / (1.0 + jnp.exp(-x))


def _dot(a, b):
    return jnp.dot(a, b, preferred_element_type=F32)


def _dot_nt(a, b):
    return lax.dot_general(a, b, (((1,), (1,)), ((), ())), preferred_element_type=F32)


def _rope_pairs(x, c, s):
    return x * c + pltpu.roll(x, LANES // 2, 1) * s


def _row_parts(xs, tm):
    bounds, lo = [], 0
    for x in xs:
        assert x.shape[0] % tm == 0
        bounds.append((lo, x.shape[0] // tm))
        lo += x.shape[0] // tm

    def index_map(part, col):
        lo, n = bounds[part]
        return lambda i, j: (jnp.clip(i - lo, 0, n - 1), jnp.where((i >= lo) & (i < lo + n), col(j), 0))

    return bounds, index_map


def _read_part(bounds, x_refs):
    i = pl.program_id(0)
    x = x_refs[-1][...]
    for part in range(len(bounds) - 2, -1, -1):
        lo, n = bounds[part]
        x = jnp.where((i >= lo) & (i < lo + n), x_refs[part][...], x)
    return x


def _norm_matmul_kernel(*refs, bounds):
    x_refs = refs[:len(bounds)]
    g_ref, w_ref, o_ref, xn_ref = refs[len(bounds):]

    def normalise(part):
        x = x_refs[part][...].astype(F32)
        xn_ref[...] = (_rms_scale(x) * g_ref[...]).astype(BF16)

    @pl.when(pl.program_id(1) == 0)
    def _():
        i = pl.program_id(0)
        for part, (lo, n) in enumerate(bounds):
            if len(bounds) == 1:
                normalise(part)
            else:
                pl.when((i >= lo) & (i < lo + n))(functools.partial(normalise, part))

    o_ref[...] = _dot(xn_ref[...], w_ref[...]).astype(o_ref.dtype)


def _norm_matmul(xs, g, w, out_dtype, tm=1024, tn=1024):
    xs = xs if isinstance(xs, (list, tuple)) else [xs]
    T, K = sum(x.shape[0] for x in xs), xs[0].shape[1]
    N = w.shape[1]
    tm, tn = _tile(min(x.shape[0] for x in xs), tm), _tile(N, tn)
    bounds, index_map = _row_parts(xs, tm)
    return pl.pallas_call(
        functools.partial(_norm_matmul_kernel, bounds=bounds),
        grid=(T // tm, N // tn),
        in_specs=[pl.BlockSpec((tm, K), index_map(p, lambda j: 0)) for p in range(len(xs))] + [
                  pl.BlockSpec((1, K), lambda i, j: (0, 0)),
                  pl.BlockSpec((K, tn), lambda i, j: (0, j))],
        out_specs=pl.BlockSpec((tm, tn), lambda i, j: (i, j)),
        out_shape=jax.ShapeDtypeStruct((T, N), out_dtype),
        scratch_shapes=[pltpu.VMEM((tm, K), BF16)],
        compiler_params=_params(("parallel", "arbitrary")),
        name="norm_matmul",
    )(*xs, g.reshape(1, K).astype(F32), w)


def _matmul_residual_kernel(a_ref, w_ref, *refs, bounds):
    x_refs, o_ref = refs[:len(bounds)], refs[len(bounds)]
    o_ref[...] = _read_part(bounds, x_refs) + _dot(a_ref[...], w_ref[...])


def _matmul_residual(a, w, xs, tm=1024, tn=512):
    xs = xs if isinstance(xs, (list, tuple)) else [xs]
    T, K = a.shape
    N = w.shape[1]
    tm, tn = _tile(min(x.shape[0] for x in xs), tm), _tile(N, tn)
    bounds, index_map = _row_parts(xs, tm)
    return pl.pallas_call(
        functools.partial(_matmul_residual_kernel, bounds=bounds),
        grid=(T // tm, N // tn),
        in_specs=[pl.BlockSpec((tm, K), lambda i, j: (i, 0)),
                  pl.BlockSpec((K, tn), lambda i, j: (0, j))] + [
                  pl.BlockSpec((tm, tn), index_map(p, lambda j: j)) for p in range(len(xs))],
        out_specs=pl.BlockSpec((tm, tn), lambda i, j: (i, j)),
        out_shape=jax.ShapeDtypeStruct((T, N), F32),
        input_output_aliases={2: 0} if len(xs) == 1 else {},
        compiler_params=_params(("parallel", "arbitrary")),
        name="matmul_residual",
    )(a, w, *xs)


def _mlp_kernel(x_ref, g_ref, w1_ref, w2_ref, o_ref, xn_ref):
    @pl.when(pl.program_id(1) == 0)
    def _():
        x = x_ref[...]
        xn_ref[...] = (_rms_scale(x) * g_ref[...]).astype(BF16)
        o_ref[...] = x

    tf = w1_ref.shape[1]
    chunk = min(MLP_FF_CHUNK, tf)
    for c in range(tf // chunk):
        cs = slice(c * chunk, (c + 1) * chunk)
        a = jnp.maximum(_dot(xn_ref[...], w1_ref[:, cs]), 0.0)
        o_ref[...] += _dot((a * a).astype(BF16), w2_ref[cs, :])


def _mlp(x, g, w1, w2, layer, row0=0, rows=None, tm=512, tf=2048):
    D = x.shape[1]
    rows = x.shape[0] if rows is None else rows
    Fd = w1.shape[2]
    tm, tf = _tile(rows, tm), _tile(Fd, tf)
    assert row0 % tm == 0
    blk0 = row0 // tm
    return pl.pallas_call(
        _mlp_kernel,
        grid=(rows // tm, Fd // tf),
        in_specs=[pl.BlockSpec((tm, D), lambda i, f: (blk0 + i, 0)),
                  pl.BlockSpec((1, D), lambda i, f: (0, 0)),
                  pl.BlockSpec((None, D, tf), lambda i, f: (layer, 0, f)),
                  pl.BlockSpec((None, tf, D), lambda i, f: (layer, f, 0))],
        out_specs=pl.BlockSpec((tm, D), lambda i, f: (i, 0)),
        out_shape=jax.ShapeDtypeStruct((rows, D), F32),
        scratch_shapes=[pltpu.VMEM((tm, D), BF16)],
        compiler_params=_params(("parallel", "arbitrary")),
        name="mlp",
    )(x, g.reshape(1, D).astype(F32), w1, w2)


def _xattn_kernel(x_ref, g_ref, wq_ref, kv_ref, gq_ref, gk_ref, wo_ref, o_ref):
    x = x_ref[0]
    xn = (_rms_scale(x) * g_ref[...]).astype(BF16)
    q = _dot(xn, wq_ref[...])
    kv = kv_ref[0].astype(F32)
    width = MEM_HEADS * MEM_HD
    outs = []
    for h in range(MEM_HEADS):
        cs = slice(h * MEM_HD, (h + 1) * MEM_HD)
        qh = (_rms_scale(q[:, cs]) * gq_ref[...]).astype(BF16)
        kh = (_rms_scale(kv[:, cs]) * gk_ref[...]).astype(BF16)
        vh = kv_ref[0, :, width + h * MEM_HD: width + (h + 1) * MEM_HD]
        s = _dot_nt(qh, kh)
        p = jnp.exp(s - jnp.max(s, axis=-1, keepdims=True))
        l = jnp.sum(p, axis=-1, keepdims=True)
        outs.append((_dot(p.astype(BF16), vh) / l).astype(BF16))
    o = jnp.concatenate(outs, axis=-1)
    o_ref[0] = x + _dot(o, wo_ref[...])


def _xattn(x, kv, g, wq, gq, gk, wo, tm=1024):
    B, L, D = x.shape
    M = kv.shape[1]
    width = MEM_HEADS * MEM_HD
    tm = _tile(L, tm)
    return pl.pallas_call(
        _xattn_kernel,
        grid=(B, L // tm),
        in_specs=[pl.BlockSpec((1, tm, D), lambda b, i: (b, i, 0)),
                  pl.BlockSpec((1, D), lambda b, i: (0, 0)),
                  pl.BlockSpec((D, width), lambda b, i: (0, 0)),
                  pl.BlockSpec((1, M, 2 * width), lambda b, i: (b, 0, 0)),
                  pl.BlockSpec((1, MEM_HD), lambda b, i: (0, 0)),
                  pl.BlockSpec((1, MEM_HD), lambda b, i: (0, 0)),
                  pl.BlockSpec((width, D), lambda b, i: (0, 0))],
        out_specs=pl.BlockSpec((1, tm, D), lambda b, i: (b, i, 0)),
        out_shape=jax.ShapeDtypeStruct((B, L, D), F32),
        input_output_aliases={0: 0},
        compiler_params=_params(("parallel", "arbitrary")),
        name="mem_xattn",
    )(x, g.reshape(1, D).astype(F32), wq, kv,
      (gq.astype(F32) * (MEM_HD ** -0.5)).reshape(1, MEM_HD), gk.reshape(1, MEM_HD).astype(F32), wo)


def _attn_kernel(q_ref, k_ref, v_ref, o_ref, *, heads, q_per_kv, dq, dv):
    def scores(g):
        kg = g // q_per_kv
        return _dot_nt(q_ref[0, :, g * dq:(g + 1) * dq], k_ref[0, :, kg * dq:(kg + 1) * dq])

    s_next = scores(0)
    for g in range(heads):
        s, s_next = s_next, (scores(g + 1) if g + 1 < heads else None)
        kg = g // q_per_kv
        v = v_ref[0, :, kg * dv:(kg + 1) * dv]
        p = jnp.exp2(s - jnp.max(s, axis=-1, keepdims=True))
        l = jnp.sum(p, axis=-1, keepdims=True)
        o_ref[0, :, g * dv:(g + 1) * dv] = (_dot(p.astype(BF16), v) / l).astype(o_ref.dtype)


def _attention(q, k, v, *, n_heads, q_per_kv, dq, dv, k_col0, v_col0, tq=512):
    B, L = q.shape[0], q.shape[1]
    tq = _tile(L, tq)
    heads = ATTN_HEADS_PER_STEP
    kvh = heads // q_per_kv
    k_block0, v_block0 = k_col0 // (kvh * dq), v_col0 // (kvh * dv)
    assert n_heads % heads == 0 and k_col0 % (kvh * dq) == 0 and v_col0 % (kvh * dv) == 0
    return pl.pallas_call(
        functools.partial(_attn_kernel, heads=heads, q_per_kv=q_per_kv, dq=dq, dv=dv),
        grid=(B, n_heads // heads, L // tq),
        in_specs=[pl.BlockSpec((1, tq, heads * dq), lambda b, h, i: (b, i, h)),
                  pl.BlockSpec((1, L, kvh * dq), lambda b, h, i: (b, 0, k_block0 + h)),
                  pl.BlockSpec((1, L, kvh * dv), lambda b, h, i: (b, 0, v_block0 + h))],
        out_specs=pl.BlockSpec((1, tq, heads * dv), lambda b, h, i: (b, i, h)),
        out_shape=jax.ShapeDtypeStruct((B, L, n_heads * dv), BF16),
        compiler_params=_params(("parallel", "parallel", "arbitrary")),
        name="softmax_attention",
    )(q, k, v)


def _retention_kernel(q_ref, k_ref, v_ref, g_ref, cos_ref, sin_ref, dec_ref, on_ref, o_ref, *, tq):
    for h in range(RET_HEADS_PER_STEP):
        qk = pl.ds(h * RET_DK, RET_DK)
        vg = pl.ds(h * RET_DV, RET_DV)
        _retention_head(q_ref.at[:, :, qk], k_ref.at[:, :, qk], v_ref.at[:, :, vg], g_ref.at[:, :, vg],
                        cos_ref, sin_ref, dec_ref.at[pl.ds(h, 1)], on_ref, o_ref.at[:, :, vg], tq=tq)


def _retention_head(q_ref, k_ref, v_ref, g_ref, cos_ref, sin_ref, dec_ref, on_ref, o_ref, *, tq):
    L = q_ref.shape[1]
    half = RET_DK // 2
    cos = cos_ref[...]
    sin = sin_ref[...]

    def rope(ref, scale):
        x1 = ref[0, :, :half].astype(F32)
        x2 = ref[0, :, half:].astype(F32)
        y = jnp.concatenate([x1 * cos - x2 * sin, x1 * sin + x2 * cos], axis=-1)
        return (y * scale).astype(BF16)

    q = rope(q_ref, RET_DK ** -0.5)
    k = rope(k_ref, 1.0)
    v = v_ref[0]
    dec = dec_ref[0]
    lg = jnp.minimum(dec, 0.0) - jnp.log1p(jnp.exp(-jnp.abs(dec)))
    lg_f = lg[0:1, 0:1]
    lg_b = lg[1:2, 0:1]
    n = L // tq
    ri = lax.broadcasted_iota(jnp.int32, (tq, tq), 0)
    diff = (ri - lax.broadcasted_iota(jnp.int32, (tq, tq), 1)).astype(F32)
    d_intra = jnp.where(diff == 0, 2.0, jnp.exp(jnp.where(diff >= 0, diff * lg_f, -diff * lg_b)))
    r = lax.broadcasted_iota(jnp.int32, (tq, RET_DK), 0).astype(F32)
    qdec = (jnp.exp((r + 1.0) * lg_f), jnp.exp((tq - r) * lg_b))
    kdec = (jnp.exp((tq - 1.0 - r) * lg_f), jnp.exp(r * lg_b))
    gc = (jnp.exp(tq * lg_f), jnp.exp(tq * lg_b))
    rows = [slice(i * tq, (i + 1) * tq) for i in range(n)]

    def kv_state(m, d):
        kd = (k[rows[m]].astype(F32) * kdec[d]).T.astype(BF16)
        return _dot(kd, v[rows[m]])

    state = [[None] * n, [None] * n]
    for i in range(1, n):
        prev = state[0][i - 1]
        state[0][i] = kv_state(i - 1, 0) if prev is None else prev * gc[0] + kv_state(i - 1, 0)
    for i in range(n - 2, -1, -1):
        nxt = state[1][i + 1]
        state[1][i] = kv_state(i + 1, 1) if nxt is None else nxt * gc[1] + kv_state(i + 1, 1)
    for i in range(n):
        qi = q[rows[i]]
        s = _dot_nt(qi, k[rows[i]]) * d_intra
        o = _dot(s.astype(BF16), v[rows[i]])
        for d in (0, 1):
            if state[d][i] is not None:
                o = o + _dot((qi.astype(F32) * qdec[d]).astype(BF16), state[d][i].astype(BF16))
        y = _rms_scale(o) * on_ref[...]
        gate = g_ref[0, rows[i], :].astype(F32)
        o_ref[0, rows[i], :] = (y * (gate * _sigmoid(gate))).astype(o_ref.dtype)


def _retention(p, cos, sin, dec, out_norm, tq=512):
    B, L, _ = p.shape
    H = RET_HEADS
    hp = RET_HEADS_PER_STEP
    tq = _tile(L, tq)
    n = H // hp
    return pl.pallas_call(
        functools.partial(_retention_kernel, tq=tq),
        grid=(n, B),
        in_specs=[pl.BlockSpec((1, L, hp * RET_DK), lambda h, b: (b, 0, h)),
                  pl.BlockSpec((1, L, hp * RET_DK), lambda h, b: (b, 0, n + h)),
                  pl.BlockSpec((1, L, hp * RET_DV), lambda h, b: (b, 0, n + h)),
                  pl.BlockSpec((1, L, hp * RET_DV), lambda h, b: (b, 0, 2 * n + h)),
                  pl.BlockSpec((L, RET_DK // 2), lambda h, b: (0, 0)),
                  pl.BlockSpec((L, RET_DK // 2), lambda h, b: (0, 0)),
                  pl.BlockSpec((hp, 2, LANES), lambda h, b: (h, 0, 0)),
                  pl.BlockSpec((1, RET_DV), lambda h, b: (0, 0))],
        out_specs=pl.BlockSpec((1, L, hp * RET_DV), lambda h, b: (b, 0, h)),
        out_shape=jax.ShapeDtypeStruct((B, L, H * RET_DV), BF16),
        compiler_params=_params(("parallel", "arbitrary")),
        name="retention",
    )(p, p, p, p, cos, sin, dec, out_norm.reshape(1, RET_DV).astype(F32))


def _hgrn_kernel(q_ref, ff_ref, fb_ref, i_ref, g_ref, lb_ref, on_ref, m_ref, o_ref,
                 of_ref, ob_ref, st_ref, qt_ref, kt_ref, ks_ref, dec_ref, *, heads):
    L = q_ref.shape[1]
    nblk = L // HG_BLOCK
    n_chunks = HG_BLOCK // HG_CHUNK
    width = heads * HG_DK
    rows = lax.broadcasted_iota(jnp.int32, (HG_BLOCK, HG_BLOCK), 0)
    cols = lax.broadcasted_iota(jnp.int32, (HG_BLOCK, HG_BLOCK), 1)
    same_chunk = (rows // HG_CHUNK) == (cols // HG_CHUNK)
    masks = (same_chunk & (cols <= rows), same_chunk & (cols >= rows))
    row_chunk = lax.broadcasted_iota(jnp.int32, (HG_BLOCK, HG_DK), 0) // HG_CHUNK
    chunk_rows = [jnp.where(row_chunk == c, 1.0, 0.0).astype(BF16) for c in range(n_chunks)]
    st_ref[...] = jnp.zeros_like(st_ref)

    def block_rows(t, d):
        r = t if d == 0 else nblk - 1 - t
        return pl.ds(pl.multiple_of(r * HG_BLOCK, HG_BLOCK), HG_BLOCK)

    def prep(t, d, slot, part):
        rs = block_rows(t, d)
        pw = width // PREP_PARTS
        cs = slice(part * pw, (part + 1) * pw)
        fz = (ff_ref if d == 0 else fb_ref)[0, rs, cs].astype(F32)
        lb = lb_ref[:, cs]
        f = lb + (1.0 - lb) * _sigmoid(fz)
        lf = jnp.log(f)
        kk = 1.0 - f
        hi = lf.astype(BF16)
        lo = (lf - hi.astype(F32)).astype(BF16)
        cb = _dot(m_ref[d], jnp.concatenate([hi, lo], axis=1))
        b = cb[:, :pw] + cb[:, pw:]
        last = [c * HG_CHUNK + (HG_CHUNK - 1 if d == 0 else 0) for c in range(n_chunks)]
        dec_rows = [jnp.exp(b[i:i + 1, :]) for i in last]
        dec = jnp.concatenate([jnp.broadcast_to(row, (HG_CHUNK, pw)) for row in dec_rows], axis=0)
        kt = kk * jnp.exp(-b)
        qt_ref[slot, d, :, cs] = (q_ref[0, rs, cs].astype(F32) * jnp.exp(b)).astype(BF16)
        kt_ref[slot, d, :, cs] = kt.astype(BF16)
        ks_ref[slot, d, :, cs] = (kt * dec).astype(BF16)
        for c in range(n_chunks):
            dec_ref[slot, d, c:c + 1, cs] = dec_rows[c]

    items = [(h, d) for h in range(heads) for d in (0, 1)]

    def half(t, slot):
        rs = [block_rows(t, d) for d in (0, 1)]
        v = {(h, d): i_ref[0, rs[d], h * HG_DK:(h + 1) * HG_DK] for h, d in items}
        qt = {(h, d): qt_ref[slot, d, :, h * HG_DK:(h + 1) * HG_DK] for h, d in items}
        s = {(h, d): _dot_nt(qt[h, d], kt_ref[slot, d, :, h * HG_DK:(h + 1) * HG_DK]) for h, d in items}
        kv = {}
        for h, d in items:
            ks = ks_ref[slot, d, :, h * HG_DK:(h + 1) * HG_DK]
            ks_cols = jnp.concatenate([ks * chunk_rows[c] for c in range(n_chunks)], axis=1)
            kv[h, d] = _dot(v[h, d].astype(F32).T.astype(BF16), ks_cols)
        t_next = jnp.minimum(t + 1, nblk - 1)
        s = {it: jnp.where(masks[it[1]], s[it], 0.0).astype(BF16) for it in items}
        o = {it: _dot(s[it], v[it]) for it in items}
        st = {(h, d): st_ref[2 * h + d] for h, d in items}
        parts = {it: [None] * n_chunks for it in items}
        for step in range(n_chunks):
            prep(t_next, step % 2, 1 - slot, step // 2)
            for h, d in items:
                c = step if d == 0 else n_chunks - 1 - step
                cr = slice(c * HG_CHUNK, (c + 1) * HG_CHUNK)
                parts[h, d][c] = o[h, d][cr] + _dot_nt(qt[h, d][cr], st[h, d].astype(BF16))
                dec = dec_ref[slot, d, c:c + 1, h * HG_DK:(h + 1) * HG_DK]
                st[h, d] = st[h, d] * dec + kv[h, d][:, c * HG_DK:(c + 1) * HG_DK]
        for h, d in items:
            st_ref[2 * h + d] = st[h, d]
        for d in (0, 1):
            out = jnp.concatenate([jnp.concatenate(parts[h, d], axis=0) for h in range(heads)], axis=1)
            (of_ref if d == 0 else ob_ref)[rs[d], :] = out

    def body(u, carry):
        half(2 * u, 0)
        half(2 * u + 1, 1)
        return carry

    assert n_chunks == 2 * PREP_PARTS
    for d in (0, 1):
        for part in range(PREP_PARTS):
            prep(jnp.int32(0), d, 0, part)
    lax.fori_loop(0, nblk // 2, body, 0)
    for h in range(heads):
        cs = slice(h * HG_DV, (h + 1) * HG_DV)
        y = _rms_scale(of_ref[:, cs] + ob_ref[:, cs]) * on_ref[...]
        gate = g_ref[0, :, cs].astype(F32)
        o_ref[0, :, cs] = (y * (gate * _sigmoid(gate))).astype(o_ref.dtype)


def _hgrn(p, lb, out_norm):
    B, L, _ = p.shape
    heads = HG_HEADS_PER_STEP
    nb = HG_HEADS // heads
    w = heads * HG_DK
    c = HG_CHUNK
    n_chunks = HG_BLOCK // HG_CHUNK
    assert (L // HG_BLOCK) % 2 == 0
    idx = jnp.arange(HG_BLOCK)
    same = (idx[:, None] // c) == (idx[None, :] // c)
    tri_f = same & (idx[None, :] <= idx[:, None])
    tri_b = same & (idx[None, :] >= idx[:, None])
    m = jnp.stack([tri_f, tri_b]).astype(BF16)
    spec = lambda off: pl.BlockSpec((1, L, w), lambda b, j: (b, 0, off * nb + j))
    return pl.pallas_call(
        functools.partial(_hgrn_kernel, heads=heads),
        grid=(B, nb),
        in_specs=[spec(0), spec(1), spec(2), spec(3), spec(4),
                  pl.BlockSpec((1, w), lambda b, j: (0, j)),
                  pl.BlockSpec((1, HG_DV), lambda b, j: (0, 0)),
                  pl.BlockSpec((2, HG_BLOCK, HG_BLOCK), lambda b, j: (0, 0, 0))],
        out_specs=pl.BlockSpec((1, L, w), lambda b, j: (b, 0, j)),
        out_shape=jax.ShapeDtypeStruct((B, L, HG_HEADS * HG_DV), BF16),
        scratch_shapes=[pltpu.VMEM((L, w), F32), pltpu.VMEM((L, w), F32),
                        pltpu.VMEM((2 * heads, HG_DV, HG_DK), F32),
                        pltpu.VMEM((2, 2, HG_BLOCK, w), BF16), pltpu.VMEM((2, 2, HG_BLOCK, w), BF16),
                        pltpu.VMEM((2, 2, HG_BLOCK, w), BF16), pltpu.VMEM((2, 2, n_chunks, w), F32)],
        compiler_params=_params(("parallel", "arbitrary")),
        name="hgrn2",
    )(p, p, p, p, p, lb.reshape(1, -1).astype(F32), out_norm.reshape(1, HG_DV).astype(F32), m)


def _mla_q_kernel(c_ref, gn_ref, w_ref, gh_ref, cos_ref, sin_ref, o_ref, xn_ref):
    @pl.when(pl.program_id(1) == 0)
    def _():
        xn_ref[...] = (_rms_scale(c_ref[...]) * gn_ref[...]).astype(BF16)

    hw = 2 * LANES
    head_dot = lambda g: _dot(xn_ref[...], w_ref[:, g * hw:(g + 1) * hw])
    y_next = head_dot(0)
    for g in range(MLA_HEADS_PER_STEP):
        y, y_next = y_next, (head_dot(g + 1) if g + 1 < MLA_HEADS_PER_STEP else None)
        inv = _inv_rms_lanes(y, MLA_NOPE + MLA_ROPE)
        gh = gh_ref[...]
        o_ref[:, g * hw:g * hw + MLA_NOPE] = (y[:, :MLA_NOPE] * inv * gh[:, :MLA_NOPE]).astype(o_ref.dtype)
        o_ref[:, g * hw + MLA_NOPE:(g + 1) * hw] = _rope_pairs(
            y[:, MLA_NOPE:] * inv * gh[:, MLA_NOPE:], cos_ref[...], sin_ref[...]).astype(o_ref.dtype)


def _mla_kv_kernel(c_ref, kr_ref, gn_ref, w_ref, gh_ref, cos_ref, sin_ref, k_ref, v_ref, xn_ref):
    @pl.when(pl.program_id(1) == 0)
    def _():
        xn_ref[...] = (_rms_scale(c_ref[...]) * gn_ref[...]).astype(BF16)

    hw = 2 * LANES
    kr = kr_ref[...]
    gh = gh_ref[...]
    head_dot = lambda g: _dot(xn_ref[...], w_ref[:, g * hw:(g + 1) * hw])
    y_next = head_dot(0)
    for g in range(MLA_HEADS_PER_STEP):
        y, y_next = y_next, (head_dot(g + 1) if g + 1 < MLA_HEADS_PER_STEP else None)
        kn = y[:, :MLA_NOPE]
        inv = _inv_rms_lanes(jnp.concatenate([kn, kr], axis=1), MLA_NOPE + MLA_ROPE)
        k_ref[:, g * hw:g * hw + MLA_NOPE] = (kn * inv * gh[:, :MLA_NOPE]).astype(k_ref.dtype)
        k_ref[:, g * hw + MLA_NOPE:(g + 1) * hw] = _rope_pairs(
            kr * inv * gh[:, MLA_NOPE:], cos_ref[...], sin_ref[...]).astype(k_ref.dtype)
        v_ref[:, g * MLA_V:(g + 1) * MLA_V] = y[:, MLA_NOPE:].astype(v_ref.dtype)


def _mla_qkv(c, L, q_norm, kv_norm, wq, wkv, gq, gk, tabs, tm=1024):
    T = c.shape[0]
    H = MLA_HEADS
    hps = MLA_HEADS_PER_STEP
    tm = _tile(L, tm)
    lt = L // tm
    hw = 2 * LANES
    tab_spec = pl.BlockSpec((tm, LANES), lambda i, h: (i % lt, 0))
    row = lambda n: pl.BlockSpec((1, n), lambda i, h: (0, 0))
    q = pl.pallas_call(
        _mla_q_kernel,
        grid=(T // tm, H // hps),
        in_specs=[pl.BlockSpec((tm, MLA_Q_RANK), lambda i, h: (i, 0)), row(MLA_Q_RANK),
                  pl.BlockSpec((MLA_Q_RANK, hps * hw), lambda i, h: (0, h)), row(hw),
                  tab_spec, tab_spec],
        out_specs=pl.BlockSpec((tm, hps * hw), lambda i, h: (i, h)),
        out_shape=jax.ShapeDtypeStruct((T, H * hw), BF16),
        scratch_shapes=[pltpu.VMEM((tm, MLA_Q_RANK), BF16)],
        compiler_params=_params(("parallel", "arbitrary")),
        name="mla_q",
    )(c, q_norm.reshape(1, -1).astype(F32), wq, gq, *tabs)
    k, v = pl.pallas_call(
        _mla_kv_kernel,
        grid=(T // tm, H // hps),
        in_specs=[pl.BlockSpec((tm, MLA_KV_RANK), lambda i, h: (i, 1)),
                  pl.BlockSpec((tm, LANES), lambda i, h: (i, (MLA_Q_RANK + MLA_KV_RANK) // LANES)),
                  row(MLA_KV_RANK),
                  pl.BlockSpec((MLA_KV_RANK, hps * hw), lambda i, h: (0, h)), row(hw),
                  tab_spec, tab_spec],
        out_specs=[pl.BlockSpec((tm, hps * hw), lambda i, h: (i, h)),
                   pl.BlockSpec((tm, hps * MLA_V), lambda i, h: (i, h))],
        out_shape=[jax.ShapeDtypeStruct((T, H * hw), BF16), jax.ShapeDtypeStruct((T, H * MLA_V), BF16)],
        scratch_shapes=[pltpu.VMEM((tm, MLA_KV_RANK), BF16)],
        compiler_params=_params(("parallel", "arbitrary")),
        name="mla_kv",
    )(c, c, kv_norm.reshape(1, -1).astype(F32), wkv, gk, *tabs)
    return q, k, v


def _head_norm_rope_kernel(x_ref, g_ref, cos_ref, sin_ref, o_ref, *, n_heads):
    for h in range(n_heads):
        cs = slice(h * LANES, (h + 1) * LANES)
        x = x_ref[:, cs].astype(F32)
        y = x * _inv_rms_lanes(x, LANES) * g_ref[:, cs]
        o_ref[:, cs] = _rope_pairs(y, cos_ref[...], sin_ref[...]).astype(o_ref.dtype)


def _head_norm_rope(p, L, n_heads, gains, tabs, tm=512):
    T = p.shape[0]
    tm = _tile(L, tm)
    lt = L // tm
    w = n_heads * LANES
    tab_spec = pl.BlockSpec((tm, LANES), lambda i: (i % lt, 0))
    return pl.pallas_call(
        functools.partial(_head_norm_rope_kernel, n_heads=n_heads),
        grid=(T // tm,),
        in_specs=[pl.BlockSpec((tm, w), lambda i: (i, 0)),
                  pl.BlockSpec((1, w), lambda i: (0, 0)),
                  tab_spec, tab_spec],
        out_specs=pl.BlockSpec((tm, w), lambda i: (i, 0)),
        out_shape=jax.ShapeDtypeStruct((T, w), BF16),
        compiler_params=_params(("parallel",)),
        name="head_norm_rope",
    )(p, gains, *tabs)


ROPE_HALF = 32


def _rope_tables(pos_a, pos_b, base):
    freqs = base ** (-jnp.arange(ROPE_HALF, dtype=F32) / ROPE_HALF)

    def cs(pos):
        ang = pos[:, None] * freqs[None, :]
        return jnp.cos(ang), jnp.sin(ang)

    ca, sa = cs(pos_a)
    cb, sb = (jnp.zeros_like(ca), jnp.zeros_like(sa)) if pos_b is None else cs(pos_b)
    return jnp.concatenate([ca, cb, ca, cb], axis=-1), jnp.concatenate([-sa, -sb, sa, sb], axis=-1)


def _spread_pairs(a):
    z = jnp.zeros(a.shape[:-1] + (ROPE_HALF,), a.dtype)
    return jnp.concatenate([a[..., :ROPE_HALF], z, a[..., ROPE_HALF:], z], axis=-1)


def _interleave_halves(a):
    q = [a[..., i * ROPE_HALF:(i + 1) * ROPE_HALF] for i in range(4)]
    return jnp.concatenate([q[0], q[2], q[1], q[3]], axis=-1)


def _trunk(xs, mem, w):
    group_batches = [x.shape[0] for x in xs]
    B, (L, D) = sum(group_batches), xs[0].shape[1:]
    T = B * L
    M = mem.shape[1]
    x = [x.reshape(-1, D) for x in xs]
    mem = mem.reshape(B * M, D)
    pos = jnp.arange(L, dtype=F32)
    sm = jax.nn.softmax(w['hg_lb'].astype(F32), axis=0)
    lb_all = jnp.cumsum(sm, axis=0) - sm[0]
    depth = w['norm_mix'].shape[0]
    for i in range(depth):
        kind, j = i % 4, i // 4
        if kind == 0:
            p = _norm_matmul(x, w['norm_mix'][i], w['ret_w_in'][j], BF16)
            half = RET_DK // 2
            ang = pos[:, None] * (RET_ROPE_BASE ** (-jnp.arange(half, dtype=F32) / half))[None, :]
            dec = jnp.broadcast_to(w['ret_decay'][j].astype(F32).T[:, :, None], (RET_HEADS, 2, LANES))
            o = _retention(p.reshape(B, L, -1), jnp.cos(ang), jnp.sin(ang), dec, w['ret_out_norm'][j])
            x = _matmul_residual(o.reshape(T, -1), w['ret_w_out'][j], x)
        elif kind == 1:
            p = _norm_matmul(x, w['norm_mix'][i], w['hg_w_in'][j], BF16, tn=2048)
            o = _hgrn(p.reshape(B, L, -1), lb_all[i], w['hg_out_norm'][j])
            x = _matmul_residual(o.reshape(T, -1), w['hg_w_out'][j], x, tm=512, tn=D)
        elif kind == 2:
            c = _norm_matmul(x, w['norm_mix'][i], w['mla_w_in'][j], F32)
            tabs = _rope_tables(pos, None, MLA_ROPE_BASE)
            q, k, v = _mla_qkv(c, L, w['mla_q_norm'][j], w['mla_kv_norm'][j], w['mla_w_qb'][j], w['mla_w_kvb'][j],
                               w['mla_gq'][j], w['mla_gk'][j], tabs)
            o = _attention(q.reshape(B, L, -1), k.reshape(B, L, -1), v.reshape(B, L, -1),
                           n_heads=MLA_HEADS, q_per_kv=1, dq=2 * LANES, dv=MLA_V, k_col0=0, v_col0=0)
            x = _matmul_residual(o.reshape(T, -1), w['mla_w_out'][j], x, tm=512, tn=D)
        else:
            p = _norm_matmul(x, w['norm_mix'][i], w['gqa_w_in'][j], BF16)
            t = jnp.arange(L)
            tabs = _rope_tables((t // GRID_W).astype(F32), (t % GRID_W).astype(F32), GQA_ROPE_BASE)
            nqk = GQA_HEADS + GQA_KV_HEADS
            qk = _head_norm_rope(p, L, nqk, w['gqa_gains'][j], tabs)
            o = _attention(qk.reshape(B, L, -1), qk.reshape(B, L, -1), p.reshape(B, L, -1),
                           n_heads=GQA_HEADS, q_per_kv=GQA_HEADS // GQA_KV_HEADS, dq=GQA_HD, dv=GQA_HD,
                           k_col0=GQA_HEADS * GQA_HD, v_col0=nqk * GQA_HD)
            x = _matmul_residual(o.reshape(T, -1), w['gqa_w_out'][j], x, tm=512, tn=D)
        kv = _norm_matmul(mem, w['norm_memtok'][i], w['mem_w_kv'][i], BF16)
        x = _xattn(x.reshape(B, L, D), kv.reshape(B, M, -1), w['norm_mem'][i], w['mem_w_q'][i],
                   w['mem_qk_norm'][i, 0], w['mem_qk_norm'][i, 1], w['mem_w_out'][i]).reshape(T, D)
        mlp_w = (w['norm_mlp'][i], w['mlp_w1'], w['mlp_w2'], i)
        if i < depth - 1:
            x = _mlp(x, *mlp_w)
    outs = []
    row0 = 0
    for nb in group_batches:
        outs.append(_mlp(x, *mlp_w, row0=row0, rows=nb * L).reshape(nb, L, D))
        row0 += nb * L
    return tuple(outs)


def _prepare_weights(w):
    out = dict(w)
    for name in ('ret_w_in', 'ret_w_out', 'hg_w_in', 'hg_w_out', 'mla_w_out', 'gqa_w_in', 'gqa_w_out',
                 'mem_w_q', 'mem_w_kv', 'mem_w_out', 'mlp_w1', 'mlp_w2', 'mla_w_kvb'):
        out[name] = w[name].astype(BF16)
    n = w['mla_w_in'].shape[0]
    lat = MLA_Q_RANK + MLA_KV_RANK
    out['mla_w_in'] = jnp.concatenate([w['mla_w_in'][..., :lat], _spread_pairs(w['mla_w_in'][..., lat:])],
                                      axis=-1).astype(BF16)
    wq = w['mla_w_qb'].reshape(n, MLA_Q_RANK, MLA_HEADS, MLA_NOPE + MLA_ROPE)
    wq = jnp.concatenate([wq[..., :MLA_NOPE], _spread_pairs(wq[..., MLA_NOPE:])], axis=-1)
    out['mla_w_qb'] = wq.reshape(n, MLA_Q_RANK, -1).astype(BF16)
    g = w['mla_qk_norm'].astype(F32)
    g = jnp.concatenate([g[..., :MLA_NOPE], _spread_pairs(g[..., MLA_NOPE:])], axis=-1)
    out['mla_gq'] = g[:, 0:1] * ((MLA_NOPE + MLA_ROPE) ** -0.5 * LOG2_E)
    out['mla_gk'] = g[:, 1:2]
    nqk = GQA_HEADS + GQA_KV_HEADS
    wi = w['gqa_w_in']
    wqk = _interleave_halves(wi[..., :nqk * GQA_HD].reshape(wi.shape[0], wi.shape[1], nqk, GQA_HD))
    out['gqa_w_in'] = jnp.concatenate([wqk.reshape(wi.shape[0], wi.shape[1], -1), wi[..., nqk * GQA_HD:]],
                                      axis=-1).astype(BF16)
    g = _interleave_halves(w['gqa_qk_norm'].astype(F32))
    out['gqa_gains'] = jnp.concatenate([jnp.tile(g[:, 0] * (GQA_HD ** -0.5 * LOG2_E), (1, GQA_HEADS)),
                                        jnp.tile(g[:, 1], (1, GQA_KV_HEADS))], axis=-1)[:, None, :]
    return out


def kernel(x_prompt, x_sample, mem_prompt, mem_sample, norm_mix, norm_mem, norm_memtok, norm_mlp, ret_w_in, ret_decay, ret_out_norm, ret_w_out, hg_w_in, hg_lb, hg_out_norm, hg_w_out, mla_w_in, mla_q_norm, mla_kv_norm, mla_w_qb, mla_w_kvb, mla_qk_norm, mla_w_out, gqa_w_in, gqa_qk_norm, gqa_w_out, mem_w_q, mem_w_kv, mem_qk_norm, mem_w_out, mlp_w1, mlp_w2):
    w = _prepare_weights(dict(
        norm_mix=norm_mix, norm_mem=norm_mem, norm_memtok=norm_memtok, norm_mlp=norm_mlp,
        ret_w_in=ret_w_in, ret_decay=ret_decay, ret_out_norm=ret_out_norm, ret_w_out=ret_w_out,
        hg_w_in=hg_w_in, hg_lb=hg_lb, hg_out_norm=hg_out_norm, hg_w_out=hg_w_out,
        mla_w_in=mla_w_in, mla_q_norm=mla_q_norm, mla_kv_norm=mla_kv_norm, mla_w_qb=mla_w_qb,
        mla_w_kvb=mla_w_kvb, mla_qk_norm=mla_qk_norm, mla_w_out=mla_w_out,
        gqa_w_in=gqa_w_in, gqa_qk_norm=gqa_qk_norm, gqa_w_out=gqa_w_out,
        mem_w_q=mem_w_q, mem_w_kv=mem_w_kv, mem_qk_norm=mem_qk_norm, mem_w_out=mem_w_out,
        mlp_w1=mlp_w1, mlp_w2=mlp_w2))
    mem = jnp.concatenate([mem_prompt, mem_sample], axis=0)
    return _trunk([x_prompt, x_sample], mem, w)
```

```python
import functools

import jax
import jax.numpy as jnp
from jax import lax
from jax.experimental import pallas as pl
from jax.experimental.pallas import tpu as pltpu

F32 = jnp.float32
BF16 = jnp.bfloat16

D_MODEL = 2048
GRID_W = 64
NORM_EPS = 1e-6
LOG2_E = 1.4426950408889634
RET_HEADS, RET_DK, RET_DV = 8, 256, 512
RET_ROPE_BASE = 10000.0
HG_HEADS, HG_DK, HG_DV, HG_CHUNK = 16, 128, 128, 32
MLA_HEADS, MLA_Q_RANK, MLA_KV_RANK, MLA_NOPE, MLA_ROPE, MLA_V = 16, 512, 512, 128, 64, 128
MLA_ROPE_BASE = 10000.0
GQA_HEADS, GQA_KV_HEADS, GQA_HD = 16, 4, 128
GQA_ROPE_BASE = 10000.0
MEM_HEADS, MEM_HD = 4, 128

LANES = 128
VMEM_LIMIT_BYTES = 56 * 2 ** 20

HG_BLOCK = 128
HG_HEADS_PER_STEP = 4
PREP_PARTS = 4
MLA_HEADS_PER_STEP = 8
ATTN_HEADS_PER_STEP = 8
RET_HEADS_PER_STEP = 1
MLP_FF_CHUNK = 1024


def _params(sem):
    return pltpu.CompilerParams(dimension_semantics=sem, vmem_limit_bytes=VMEM_LIMIT_BYTES)


def _tile(n, pref):
    return pref if n % pref == 0 else n


def _rms_scale(x, width=None):
    width = x.shape[-1] if width is None else width
    ms = jnp.sum(x * x, axis=-1, keepdims=True) * (1.0 / width)
    return x * lax.rsqrt(ms + NORM_EPS)


def _inv_rms_lanes(x, width):
    sq = (x * x).astype(BF16)
    ss = _dot(sq, jnp.ones((x.shape[-1], LANES), BF16))
    return lax.rsqrt(ss * (1.0 / width) + NORM_EPS)


def _sigmoid(x):
    return 1.0 / (1.0 + jnp.exp(-x))


def _dot(a, b):
    return jnp.dot(a, b, preferred_element_type=F32)


def _dot_nt(a, b):
    return lax.dot_general(a, b, (((1,), (1,)), ((), ())), preferred_element_type=F32)


def _rope_pairs(x, c, s):
    return x * c + pltpu.roll(x, LANES // 2, 1) * s


def _row_parts(xs, tm):
    bounds, lo = [], 0
    for x in xs:
        assert x.shape[0] % tm == 0
        bounds.append((lo, x.shape[0] // tm))
        lo += x.shape[0] // tm

    def index_map(part, col):
        lo, n = bounds[part]
        return lambda i, j: (jnp.clip(i - lo, 0, n - 1), jnp.where((i >= lo) & (i < lo + n), col(j), 0))

    return bounds, index_map


def _read_part(bounds, x_refs):
    i = pl.program_id(0)
    x = x_refs[-1][...]
    for part in range(len(bounds) - 2, -1, -1):
        lo, n = bounds[part]
        x = jnp.where((i >= lo) & (i < lo + n), x_refs[part][...], x)
    return x


def _norm_matmul_kernel(*refs, bounds):
    x_refs = refs[:len(bounds)]
    g_ref, w_ref, o_ref, xn_ref = refs[len(bounds):]

    def normalise(part):
        x = x_refs[part][...].astype(F32)
        xn_ref[...] = (_rms_scale(x) * g_ref[...]).astype(BF16)

    @pl.when(pl.program_id(1) == 0)
    def _():
        i = pl.program_id(0)
        for part, (lo, n) in enumerate(bounds):
            if len(bounds) == 1:
                normalise(part)
            else:
                pl.when((i >= lo) & (i < lo + n))(functools.partial(normalise, part))

    o_ref[...] = _dot(xn_ref[...], w_ref[...]).astype(o_ref.dtype)


def _norm_matmul(xs, g, w, out_dtype, tm=1024, tn=1024):
    xs = xs if isinstance(xs, (list, tuple)) else [xs]
    T, K = sum(x.shape[0] for x in xs), xs[0].shape[1]
    N = w.shape[1]
    tm, tn = _tile(min(x.shape[0] for x in xs), tm), _tile(N, tn)
    bounds, index_map = _row_parts(xs, tm)
    return pl.pallas_call(
        functools.partial(_norm_matmul_kernel, bounds=bounds),
        grid=(T // tm, N // tn),
        in_specs=[pl.BlockSpec((tm, K), index_map(p, lambda j: 0)) for p in range(len(xs))] + [
                  pl.BlockSpec((1, K), lambda i, j: (0, 0)),
                  pl.BlockSpec((K, tn), lambda i, j: (0, j))],
        out_specs=pl.BlockSpec((tm, tn), lambda i, j: (i, j)),
        out_shape=jax.ShapeDtypeStruct((T, N), out_dtype),
        scratch_shapes=[pltpu.VMEM((tm, K), BF16)],
        compiler_params=_params(("parallel", "arbitrary")),
        name="norm_matmul",
    )(*xs, g.reshape(1, K).astype(F32), w)


def _matmul_residual_kernel(a_ref, w_ref, *refs, bounds):
    x_refs, o_ref = refs[:len(bounds)], refs[len(bounds)]
    o_ref[...] = _read_part(bounds, x_refs) + _dot(a_ref[...], w_ref[...])


def _matmul_residual(a, w, xs, tm=1024, tn=512):
    xs = xs if isinstance(xs, (list, tuple)) else [xs]
    T, K = a.shape
    N = w.shape[1]
    tm, tn = _tile(min(x.shape[0] for x in xs), tm), _tile(N, tn)
    bounds, index_map = _row_parts(xs, tm)
    return pl.pallas_call(
        functools.partial(_matmul_residual_kernel, bounds=bounds),
        grid=(T // tm, N // tn),
        in_specs=[pl.BlockSpec((tm, K), lambda i, j: (i, 0)),
                  pl.BlockSpec((K, tn), lambda i, j: (0, j))] + [
                  pl.BlockSpec((tm, tn), index_map(p, lambda j: j)) for p in range(len(xs))],
        out_specs=pl.BlockSpec((tm, tn), lambda i, j: (i, j)),
        out_shape=jax.ShapeDtypeStruct((T, N), F32),
        input_output_aliases={2: 0} if len(xs) == 1 else {},
        compiler_params=_params(("parallel", "arbitrary")),
        name="matmul_residual",
    )(a, w, *xs)


def _mlp_kernel(x_ref, g_ref, w1_ref, w2_ref, o_ref, xn_ref):
    @pl.when(pl.program_id(1) == 0)
    def _():
        x = x_ref[...]
        xn_ref[...] = (_rms_scale(x) * g_ref[...]).astype(BF16)
        o_ref[...] = x

    tf = w1_ref.shape[1]
    chunk = min(MLP_FF_CHUNK, tf)
    for c in range(tf // chunk):
        cs = slice(c * chunk, (c + 1) * chunk)
        a = jnp.maximum(_dot(xn_ref[...], w1_ref[:, cs]), 0.0)
        o_ref[...] += _dot((a * a).astype(BF16), w2_ref[cs, :])


def _mlp(x, g, w1, w2, layer, row0=0, rows=None, tm=512, tf=2048):
    D = x.shape[1]
    rows = x.shape[0] if rows is None else rows
    Fd = w1.shape[2]
    tm, tf = _tile(rows, tm), _tile(Fd, tf)
    assert row0 % tm == 0
    blk0 = row0 // tm
    return pl.pallas_call(
        _mlp_kernel,
        grid=(rows // tm, Fd // tf),
        in_specs=[pl.BlockSpec((tm, D), lambda i, f: (blk0 + i, 0)),
                  pl.BlockSpec((1, D), lambda i, f: (0, 0)),
                  pl.BlockSpec((None, D, tf), lambda i, f: (layer, 0, f)),
                  pl.BlockSpec((None, tf, D), lambda i, f: (layer, f, 0))],
        out_specs=pl.BlockSpec((tm, D), lambda i, f: (i, 0)),
        out_shape=jax.ShapeDtypeStruct((rows, D), F32),
        scratch_shapes=[pltpu.VMEM((tm, D), BF16)],
        compiler_params=_params(("parallel", "arbitrary")),
        name="mlp",
    )(x, g.reshape(1, D).astype(F32), w1, w2)


def _xattn_kernel(x_ref, g_ref, wq_ref, kv_ref, gq_ref, gk_ref, wo_ref, o_ref):
    x = x_ref[0]
    xn = (_rms_scale(x) * g_ref[...]).astype(BF16)
    q = _dot(xn, wq_ref[...])
    kv = kv_ref[0].astype(F32)
    width = MEM_HEADS * MEM_HD
    outs = []
    for h in range(MEM_HEADS):
        cs = slice(h * MEM_HD, (h + 1) * MEM_HD)
        qh = (_rms_scale(q[:, cs]) * gq_ref[...]).astype(BF16)
        kh = (_rms_scale(kv[:, cs]) * gk_ref[...]).astype(BF16)
        vh = kv_ref[0, :, width + h * MEM_HD: width + (h + 1) * MEM_HD]
        s = _dot_nt(qh, kh)
        p = jnp.exp(s - jnp.max(s, axis=-1, keepdims=True))
        l = jnp.sum(p, axis=-1, keepdims=True)
        outs.append((_dot(p.astype(BF16), vh) / l).astype(BF16))
    o = jnp.concatenate(outs, axis=-1)
    o_ref[0] = x + _dot(o, wo_ref[...])


def _xattn(x, kv, g, wq, gq, gk, wo, tm=1024):
    B, L, D = x.shape
    M = kv.shape[1]
    width = MEM_HEADS * MEM_HD
    tm = _tile(L, tm)
    return pl.pallas_call(
        _xattn_kernel,
        grid=(B, L // tm),
        in_specs=[pl.BlockSpec((1, tm, D), lambda b, i: (b, i, 0)),
                  pl.BlockSpec((1, D), lambda b, i: (0, 0)),
                  pl.BlockSpec((D, width), lambda b, i: (0, 0)),
                  pl.BlockSpec((1, M, 2 * width), lambda b, i: (b, 0, 0)),
                  pl.BlockSpec((1, MEM_HD), lambda b, i: (0, 0)),
                  pl.BlockSpec((1, MEM_HD), lambda b, i: (0, 0)),
                  pl.BlockSpec((width, D), lambda b, i: (0, 0))],
        out_specs=pl.BlockSpec((1, tm, D), lambda b, i: (b, i, 0)),
        out_shape=jax.ShapeDtypeStruct((B, L, D), F32),
        input_output_aliases={0: 0},
        compiler_params=_params(("parallel", "arbitrary")),
        name="mem_xattn",
    )(x, g.reshape(1, D).astype(F32), wq, kv,
      (gq.astype(F32) * (MEM_HD ** -0.5)).reshape(1, MEM_HD), gk.reshape(1, MEM_HD).astype(F32), wo)


def _attn_kernel(q_ref, k_ref, v_ref, o_ref, *, heads, q_per_kv, dq, dv):
    def scores(g):
        kg = g // q_per_kv
        return _dot_nt(q_ref[0, :, g * dq:(g + 1) * dq], k_ref[0, :, kg * dq:(kg + 1) * dq])

    s_next = scores(0)
    for g in range(heads):
        s, s_next = s_next, (scores(g + 1) if g + 1 < heads else None)
        kg = g // q_per_kv
        v = v_ref[0, :, kg * dv:(kg + 1) * dv]
        p = jnp.exp2(s - jnp.max(s, axis=-1, keepdims=True))
        l = jnp.sum(p, axis=-1, keepdims=True)
        o_ref[0, :, g * dv:(g + 1) * dv] = (_dot(p.astype(BF16), v) / l).astype(o_ref.dtype)


def _attention(q, k, v, *, n_heads, q_per_kv, dq, dv, k_col0, v_col0, tq=512):
    B, L = q.shape[0], q.shape[1]
    tq = _tile(L, tq)
    heads = ATTN_HEADS_PER_STEP
    kvh = heads // q_per_kv
    k_block0, v_block0 = k_col0 // (kvh * dq), v_col0 // (kvh * dv)
    assert n_heads % heads == 0 and k_col0 % (kvh * dq) == 0 and v_col0 % (kvh * dv) == 0
    return pl.pallas_call(
        functools.partial(_attn_kernel, heads=heads, q_per_kv=q_per_kv, dq=dq, dv=dv),
        grid=(B, n_heads // heads, L // tq),
        in_specs=[pl.BlockSpec((1, tq, heads * dq), lambda b, h, i: (b, i, h)),
                  pl.BlockSpec((1, L, kvh * dq), lambda b, h, i: (b, 0, k_block0 + h)),
                  pl.BlockSpec((1, L, kvh * dv), lambda b, h, i: (b, 0, v_block0 + h))],
        out_specs=pl.BlockSpec((1, tq, heads * dv), lambda b, h, i: (b, i, h)),
        out_shape=jax.ShapeDtypeStruct((B, L, n_heads * dv), BF16),
        compiler_params=_params(("parallel", "parallel", "arbitrary")),
        name="softmax_attention",
    )(q, k, v)


def _retention_kernel(q_ref, k_ref, v_ref, g_ref, cos_ref, sin_ref, dec_ref, on_ref, o_ref, *, tq):
    for h in range(RET_HEADS_PER_STEP):
        qk = pl.ds(h * RET_DK, RET_DK)
        vg = pl.ds(h * RET_DV, RET_DV)
        _retention_head(q_ref.at[:, :, qk], k_ref.at[:, :, qk], v_ref.at[:, :, vg], g_ref.at[:, :, vg],
                        cos_ref, sin_ref, dec_ref.at[pl.ds(h, 1)], on_ref, o_ref.at[:, :, vg], tq=tq)


def _retention_head(q_ref, k_ref, v_ref, g_ref, cos_ref, sin_ref, dec_ref, on_ref, o_ref, *, tq):
    L = q_ref.shape[1]
    half = RET_DK // 2
    cos = cos_ref[...]
    sin = sin_ref[...]

    def rope(ref, scale):
        x1 = ref[0, :, :half].astype(F32)
        x2 = ref[0, :, half:].astype(F32)
        y = jnp.concatenate([x1 * cos - x2 * sin, x1 * sin + x2 * cos], axis=-1)
        return (y * scale).astype(BF16)

    q = rope(q_ref, RET_DK ** -0.5)
    k = rope(k_ref, 1.0)
    v = v_ref[0]
    dec = dec_ref[0]
    lg = jnp.minimum(dec, 0.0) - jnp.log1p(jnp.exp(-jnp.abs(dec)))
    lg_f = lg[0:1, 0:1]
    lg_b = lg[1:2, 0:1]
    n = L // tq
    ri = lax.broadcasted_iota(jnp.int32, (tq, tq), 0)
    diff = (ri - lax.broadcasted_iota(jnp.int32, (tq, tq), 1)).astype(F32)
    d_intra = jnp.where(diff == 0, 2.0, jnp.exp(jnp.where(diff >= 0, diff * lg_f, -diff * lg_b)))
    r = lax.broadcasted_iota(jnp.int32, (tq, RET_DK), 0).astype(F32)
    qdec = (jnp.exp((r + 1.0) * lg_f), jnp.exp((tq - r) * lg_b))
    kdec = (jnp.exp((tq - 1.0 - r) * lg_f), jnp.exp(r * lg_b))
    gc = (jnp.exp(tq * lg_f), jnp.exp(tq * lg_b))
    rows = [slice(i * tq, (i + 1) * tq) for i in range(n)]

    def kv_state(m, d):
        kd = (k[rows[m]].astype(F32) * kdec[d]).T.astype(BF16)
        return _dot(kd, v[rows[m]])

    state = [[None] * n, [None] * n]
    for i in range(1, n):
        prev = state[0][i - 1]
        state[0][i] = kv_state(i - 1, 0) if prev is None else prev * gc[0] + kv_state(i - 1, 0)
    for i in range(n - 2, -1, -1):
        nxt = state[1][i + 1]
        state[1][i] = kv_state(i + 1, 1) if nxt is None else nxt * gc[1] + kv_state(i + 1, 1)
    for i in range(n):
        qi = q[rows[i]]
        s = _dot_nt(qi, k[rows[i]]) * d_intra
        o = _dot(s.astype(BF16), v[rows[i]])
        for d in (0, 1):
            if state[d][i] is not None:
                o = o + _dot((qi.astype(F32) * qdec[d]).astype(BF16), state[d][i].astype(BF16))
        y = _rms_scale(o) * on_ref[...]
        gate = g_ref[0, rows[i], :].astype(F32)
        o_ref[0, rows[i], :] = (y * (gate * _sigmoid(gate))).astype(o_ref.dtype)


def _retention(p, cos, sin, dec, out_norm, tq=512):
    B, L, _ = p.shape
    H = RET_HEADS
    hp = RET_HEADS_PER_STEP
    tq = _tile(L, tq)
    n = H // hp
    return pl.pallas_call(
        functools.partial(_retention_kernel, tq=tq),
        grid=(n, B),
        in_specs=[pl.BlockSpec((1, L, hp * RET_DK), lambda h, b: (b, 0, h)),
                  pl.BlockSpec((1, L, hp * RET_DK), lambda h, b: (b, 0, n + h)),
                  pl.BlockSpec((1, L, hp * RET_DV), lambda h, b: (b, 0, n + h)),
                  pl.BlockSpec((1, L, hp * RET_DV), lambda h, b: (b, 0, 2 * n + h)),
                  pl.BlockSpec((L, RET_DK // 2), lambda h, b: (0, 0)),
                  pl.BlockSpec((L, RET_DK // 2), lambda h, b: (0, 0)),
                  pl.BlockSpec((hp, 2, LANES), lambda h, b: (h, 0, 0)),
                  pl.BlockSpec((1, RET_DV), lambda h, b: (0, 0))],
        out_specs=pl.BlockSpec((1, L, hp * RET_DV), lambda h, b: (b, 0, h)),
        out_shape=jax.ShapeDtypeStruct((B, L, H * RET_DV), BF16),
        compiler_params=_params(("parallel", "arbitrary")),
        name="retention",
    )(p, p, p, p, cos, sin, dec, out_norm.reshape(1, RET_DV).astype(F32))


def _hgrn_kernel(q_ref, ff_ref, fb_ref, i_ref, g_ref, lb_ref, on_ref, m_ref, o_ref,
                 of_ref, ob_ref, st_ref, qt_ref, kt_ref, ks_ref, dec_ref, *, heads):
    L = q_ref.shape[1]
    nblk = L // HG_BLOCK
    n_chunks = HG_BLOCK // HG_CHUNK
    width = heads * HG_DK
    rows = lax.broadcasted_iota(jnp.int32, (HG_BLOCK, HG_BLOCK), 0)
    cols = lax.broadcasted_iota(jnp.int32, (HG_BLOCK, HG_BLOCK), 1)
    same_chunk = (rows // HG_CHUNK) == (cols // HG_CHUNK)
    masks = (same_chunk & (cols <= rows), same_chunk & (cols >= rows))
    row_chunk = lax.broadcasted_iota(jnp.int32, (HG_BLOCK, HG_DK), 0) // HG_CHUNK
    chunk_rows = [jnp.where(row_chunk == c, 1.0, 0.0).astype(BF16) for c in range(n_chunks)]
    st_ref[...] = jnp.zeros_like(st_ref)

    def block_rows(t, d):
        r = t if d == 0 else nblk - 1 - t
        return pl.ds(pl.multiple_of(r * HG_BLOCK, HG_BLOCK), HG_BLOCK)

    def prep(t, d, slot, part):
        rs = block_rows(t, d)
        pw = width // PREP_PARTS
        cs = slice(part * pw, (part + 1) * pw)
        fz = (ff_ref if d == 0 else fb_ref)[0, rs, cs].astype(F32)
        lb = lb_ref[:, cs]
        f = lb + (1.0 - lb) * _sigmoid(fz)
        lf = jnp.log(f)
        kk = 1.0 - f
        hi = lf.astype(BF16)
        lo = (lf - hi.astype(F32)).astype(BF16)
        cb = _dot(m_ref[d], jnp.concatenate([hi, lo], axis=1))
        b = cb[:, :pw] + cb[:, pw:]
        last = [c * HG_CHUNK + (HG_CHUNK - 1 if d == 0 else 0) for c in range(n_chunks)]
        dec_rows = [jnp.exp(b[i:i + 1, :]) for i in last]
        dec = jnp.concatenate([jnp.broadcast_to(row, (HG_CHUNK, pw)) for row in dec_rows], axis=0)
        kt = kk * jnp.exp(-b)
        qt_ref[slot, d, :, cs] = (q_ref[0, rs, cs].astype(F32) * jnp.exp(b)).astype(BF16)
        kt_ref[slot, d, :, cs] = kt.astype(BF16)
        ks_ref[slot, d, :, cs] = (kt * dec).astype(BF16)
        for c in range(n_chunks):
            dec_ref[slot, d, c:c + 1, cs] = dec_rows[c]

    items = [(h, d) for h in range(heads) for d in (0, 1)]
    per_step = 2 * PREP_PARTS // n_chunks
    stride = len(items) // per_step
    assert per_step * n_chunks == 2 * PREP_PARTS and stride * per_step == len(items)

    def half(t, slot):
        rs = [block_rows(t, d) for d in (0, 1)]
        v = {(h, d): i_ref[0, rs[d], h * HG_DK:(h + 1) * HG_DK] for h, d in items}
        qt = {(h, d): qt_ref[slot, d, :, h * HG_DK:(h + 1) * HG_DK] for h, d in items}
        s = {(h, d): _dot_nt(qt[h, d], kt_ref[slot, d, :, h * HG_DK:(h + 1) * HG_DK]) for h, d in items}
        kv = {}
        for h, d in items:
            ks = ks_ref[slot, d, :, h * HG_DK:(h + 1) * HG_DK]
            ks_cols = jnp.concatenate([ks * chunk_rows[c] for c in range(n_chunks)], axis=1)
            kv[h, d] = _dot(v[h, d].astype(F32).T.astype(BF16), ks_cols)
        t_next = jnp.minimum(t + 1, nblk - 1)
        s = {it: jnp.where(masks[it[1]], s[it], 0.0).astype(BF16) for it in items}
        o = {it: _dot(s[it], v[it]) for it in items}
        st = {(h, d): st_ref[2 * h + d] for h, d in items}
        parts = {it: [None] * n_chunks for it in items}
        for step in range(n_chunks):
            for idx, (h, d) in enumerate(items):
                if idx % stride == 0:
                    piece = step * per_step + idx // stride
                    prep(t_next, piece % 2, 1 - slot, piece // 2)
                c = step if d == 0 else n_chunks - 1 - step
                cr = slice(c * HG_CHUNK, (c + 1) * HG_CHUNK)
                parts[h, d][c] = o[h, d][cr] + _dot_nt(qt[h, d][cr], st[h, d].astype(BF16))
                dec = dec_ref[slot, d, c:c + 1, h * HG_DK:(h + 1) * HG_DK]
                st[h, d] = st[h, d] * dec + kv[h, d][:, c * HG_DK:(c + 1) * HG_DK]
        for h, d in items:
            st_ref[2 * h + d] = st[h, d]
        for d in (0, 1):
            out = jnp.concatenate([jnp.concatenate(parts[h, d], axis=0) for h in range(heads)], axis=1)
            (of_ref if d == 0 else ob_ref)[rs[d], :] = out

    def body(u, carry):
        half(2 * u, 0)
        half(2 * u + 1, 1)
        return carry

    for d in (0, 1):
        for part in range(PREP_PARTS):
            prep(jnp.int32(0), d, 0, part)
    lax.fori_loop(0, nblk // 2, body, 0)
    for h in range(heads):
        cs = slice(h * HG_DV, (h + 1) * HG_DV)
        y = _rms_scale(of_ref[:, cs] + ob_ref[:, cs]) * on_ref[...]
        gate = g_ref[0, :, cs].astype(F32)
        o_ref[0, :, cs] = (y * (gate * _sigmoid(gate))).astype(o_ref.dtype)


def _hgrn(p, lb, out_norm):
    B, L, _ = p.shape
    heads = HG_HEADS_PER_STEP
    nb = HG_HEADS // heads
    w = heads * HG_DK
    c = HG_CHUNK
    n_chunks = HG_BLOCK // HG_CHUNK
    assert (L // HG_BLOCK) % 2 == 0
    idx = jnp.arange(HG_BLOCK)
    same = (idx[:, None] // c) == (idx[None, :] // c)
    tri_f = same & (idx[None, :] <= idx[:, None])
    tri_b = same & (idx[None, :] >= idx[:, None])
    m = jnp.stack([tri_f, tri_b]).astype(BF16)
    spec = lambda off: pl.BlockSpec((1, L, w), lambda b, j: (b, 0, off * nb + j))
    return pl.pallas_call(
        functools.partial(_hgrn_kernel, heads=heads),
        grid=(B, nb),
        in_specs=[spec(0), spec(1), spec(2), spec(3), spec(4),
                  pl.BlockSpec((1, w), lambda b, j: (0, j)),
                  pl.BlockSpec((1, HG_DV), lambda b, j: (0, 0)),
                  pl.BlockSpec((2, HG_BLOCK, HG_BLOCK), lambda b, j: (0, 0, 0))],
        out_specs=pl.BlockSpec((1, L, w), lambda b, j: (b, 0, j)),
        out_shape=jax.ShapeDtypeStruct((B, L, HG_HEADS * HG_DV), BF16),
        scratch_shapes=[pltpu.VMEM((L, w), F32), pltpu.VMEM((L, w), F32),
                        pltpu.VMEM((2 * heads, HG_DV, HG_DK), F32),
                        pltpu.VMEM((2, 2, HG_BLOCK, w), BF16), pltpu.VMEM((2, 2, HG_BLOCK, w), BF16),
                        pltpu.VMEM((2, 2, HG_BLOCK, w), BF16), pltpu.VMEM((2, 2, n_chunks, w), F32)],
        compiler_params=_params(("parallel", "arbitrary")),
        name="hgrn2",
    )(p, p, p, p, p, lb.reshape(1, -1).astype(F32), out_norm.reshape(1, HG_DV).astype(F32), m)


def _mla_q_kernel(c_ref, gn_ref, w_ref, gh_ref, cos_ref, sin_ref, o_ref, xn_ref):
    @pl.when(pl.program_id(1) == 0)
    def _():
        xn_ref[...] = (_rms_scale(c_ref[...]) * gn_ref[...]).astype(BF16)

    hw = 2 * LANES
    head_dot = lambda g: _dot(xn_ref[...], w_ref[:, g * hw:(g + 1) * hw])
    y_next = head_dot(0)
    for g in range(MLA_HEADS_PER_STEP):
        y, y_next = y_next, (head_dot(g + 1) if g + 1 < MLA_HEADS_PER_STEP else None)
        inv = _inv_rms_lanes(y, MLA_NOPE + MLA_ROPE)
        gh = gh_ref[...]
        o_ref[:, g * hw:g * hw + MLA_NOPE] = (y[:, :MLA_NOPE] * inv * gh[:, :MLA_NOPE]).astype(o_ref.dtype)
        o_ref[:, g * hw + MLA_NOPE:(g + 1) * hw] = _rope_pairs(
            y[:, MLA_NOPE:] * inv * gh[:, MLA_NOPE:], cos_ref[...], sin_ref[...]).astype(o_ref.dtype)


def _mla_kv_kernel(c_ref, kr_ref, gn_ref, w_ref, gh_ref, cos_ref, sin_ref, k_ref, v_ref, xn_ref):
    @pl.when(pl.program_id(1) == 0)
    def _():
        xn_ref[...] = (_rms_scale(c_ref[...]) * gn_ref[...]).astype(BF16)

    hw = 2 * LANES
    kr = kr_ref[...]
    gh = gh_ref[...]
    head_dot = lambda g: _dot(xn_ref[...], w_ref[:, g * hw:(g + 1) * hw])
    y_next = head_dot(0)
    for g in range(MLA_HEADS_PER_STEP):
        y, y_next = y_next, (head_dot(g + 1) if g + 1 < MLA_HEADS_PER_STEP else None)
        kn = y[:, :MLA_NOPE]
        inv = _inv_rms_lanes(jnp.concatenate([kn, kr], axis=1), MLA_NOPE + MLA_ROPE)
        k_ref[:, g * hw:g * hw + MLA_NOPE] = (kn * inv * gh[:, :MLA_NOPE]).astype(k_ref.dtype)
        k_ref[:, g * hw + MLA_NOPE:(g + 1) * hw] = _rope_pairs(
            kr * inv * gh[:, MLA_NOPE:], cos_ref[...], sin_ref[...]).astype(k_ref.dtype)
        v_ref[:, g * MLA_V:(g + 1) * MLA_V] = y[:, MLA_NOPE:].astype(v_ref.dtype)


def _mla_qkv(c, L, q_norm, kv_norm, wq, wkv, gq, gk, tabs, tm=1024):
    T = c.shape[0]
    H = MLA_HEADS
    hps = MLA_HEADS_PER_STEP
    tm = _tile(L, tm)
    lt = L // tm
    hw = 2 * LANES
    tab_spec = pl.BlockSpec((tm, LANES), lambda i, h: (i % lt, 0))
    row = lambda n: pl.BlockSpec((1, n), lambda i, h: (0, 0))
    q = pl.pallas_call(
        _mla_q_kernel,
        grid=(T // tm, H // hps),
        in_specs=[pl.BlockSpec((tm, MLA_Q_RANK), lambda i, h: (i, 0)), row(MLA_Q_RANK),
                  pl.BlockSpec((MLA_Q_RANK, hps * hw), lambda i, h: (0, h)), row(hw),
                  tab_spec, tab_spec],
        out_specs=pl.BlockSpec((tm, hps * hw), lambda i, h: (i, h)),
        out_shape=jax.ShapeDtypeStruct((T, H * hw), BF16),
        scratch_shapes=[pltpu.VMEM((tm, MLA_Q_RANK), BF16)],
        compiler_params=_params(("parallel", "arbitrary")),
        name="mla_q",
    )(c, q_norm.reshape(1, -1).astype(F32), wq, gq, *tabs)
    k, v = pl.pallas_call(
        _mla_kv_kernel,
        grid=(T // tm, H // hps),
        in_specs=[pl.BlockSpec((tm, MLA_KV_RANK), lambda i, h: (i, 1)),
                  pl.BlockSpec((tm, LANES), lambda i, h: (i, (MLA_Q_RANK + MLA_KV_RANK) // LANES)),
                  row(MLA_KV_RANK),
                  pl.BlockSpec((MLA_KV_RANK, hps * hw), lambda i, h: (0, h)), row(hw),
                  tab_spec, tab_spec],
        out_specs=[pl.BlockSpec((tm, hps * hw), lambda i, h: (i, h)),
                   pl.BlockSpec((tm, hps * MLA_V), lambda i, h: (i, h))],
        out_shape=[jax.ShapeDtypeStruct((T, H * hw), BF16), jax.ShapeDtypeStruct((T, H * MLA_V), BF16)],
        scratch_shapes=[pltpu.VMEM((tm, MLA_KV_RANK), BF16)],
        compiler_params=_params(("parallel", "arbitrary")),
        name="mla_kv",
    )(c, c, kv_norm.reshape(1, -1).astype(F32), wkv, gk, *tabs)
    return q, k, v


def _head_norm_rope_kernel(x_ref, g_ref, cos_ref, sin_ref, o_ref, *, n_heads):
    for h in range(n_heads):
        cs = slice(h * LANES, (h + 1) * LANES)
        x = x_ref[:, cs].astype(F32)
        y = x * _inv_rms_lanes(x, LANES) * g_ref[:, cs]
        o_ref[:, cs] = _rope_pairs(y, cos_ref[...], sin_ref[...]).astype(o_ref.dtype)


def _head_norm_rope(p, L, n_heads, gains, tabs, tm=512):
    T = p.shape[0]
    tm = _tile(L, tm)
    lt = L // tm
    w = n_heads * LANES
    tab_spec = pl.BlockSpec((tm, LANES), lambda i: (i % lt, 0))
    return pl.pallas_call(
        functools.partial(_head_norm_rope_kernel, n_heads=n_heads),
        grid=(T // tm,),
        in_specs=[pl.BlockSpec((tm, w), lambda i: (i, 0)),
                  pl.BlockSpec((1, w), lambda i: (0, 0)),
                  tab_spec, tab_spec],
        out_specs=pl.BlockSpec((tm, w), lambda i: (i, 0)),
        out_shape=jax.ShapeDtypeStruct((T, w), BF16),
        compiler_params=_params(("parallel",)),
        name="head_norm_rope",
    )(p, gains, *tabs)


ROPE_HALF = 32


def _rope_tables(pos_a, pos_b, base):
    freqs = base ** (-jnp.arange(ROPE_HALF, dtype=F32) / ROPE_HALF)

    def cs(pos):
        ang = pos[:, None] * freqs[None, :]
        return jnp.cos(ang), jnp.sin(ang)

    ca, sa = cs(pos_a)
    cb, sb = (jnp.zeros_like(ca), jnp.zeros_like(sa)) if pos_b is None else cs(pos_b)
    return jnp.concatenate([ca, cb, ca, cb], axis=-1), jnp.concatenate([-sa, -sb, sa, sb], axis=-1)


def _spread_pairs(a):
    z = jnp.zeros(a.shape[:-1] + (ROPE_HALF,), a.dtype)
    return jnp.concatenate([a[..., :ROPE_HALF], z, a[..., ROPE_HALF:], z], axis=-1)


def _interleave_halves(a):
    q = [a[..., i * ROPE_HALF:(i + 1) * ROPE_HALF] for i in range(4)]
    return jnp.concatenate([q[0], q[2], q[1], q[3]], axis=-1)


def _trunk(xs, mem, w):
    group_batches = [x.shape[0] for x in xs]
    B, (L, D) = sum(group_batches), xs[0].shape[1:]
    T = B * L
    M = mem.shape[1]
    x = [x.reshape(-1, D) for x in xs]
    mem = mem.reshape(B * M, D)
    pos = jnp.arange(L, dtype=F32)
    sm = jax.nn.softmax(w['hg_lb'].astype(F32), axis=0)
    lb_all = jnp.cumsum(sm, axis=0) - sm[0]
    depth = w['norm_mix'].shape[0]
    for i in range(depth):
        kind, j = i % 4, i // 4
        if kind == 0:
            p = _norm_matmul(x, w['norm_mix'][i], w['ret_w_in'][j], BF16)
            half = RET_DK // 2
            ang = pos[:, None] * (RET_ROPE_BASE ** (-jnp.arange(half, dtype=F32) / half))[None, :]
            dec = jnp.broadcast_to(w['ret_decay'][j].astype(F32).T[:, :, None], (RET_HEADS, 2, LANES))
            o = _retention(p.reshape(B, L, -1), jnp.cos(ang), jnp.sin(ang), dec, w['ret_out_norm'][j])
            x = _matmul_residual(o.reshape(T, -1), w['ret_w_out'][j], x)
        elif kind == 1:
            p = _norm_matmul(x, w['norm_mix'][i], w['hg_w_in'][j], BF16, tn=2048)
            o = _hgrn(p.reshape(B, L, -1), lb_all[i], w['hg_out_norm'][j])
            x = _matmul_residual(o.reshape(T, -1), w['hg_w_out'][j], x, tm=512, tn=D)
        elif kind == 2:
            c = _norm_matmul(x, w['norm_mix'][i], w['mla_w_in'][j], F32)
            tabs = _rope_tables(pos, None, MLA_ROPE_BASE)
            q, k, v = _mla_qkv(c, L, w['mla_q_norm'][j], w['mla_kv_norm'][j], w['mla_w_qb'][j], w['mla_w_kvb'][j],
                               w['mla_gq'][j], w['mla_gk'][j], tabs)
            o = _attention(q.reshape(B, L, -1), k.reshape(B, L, -1), v.reshape(B, L, -1),
                           n_heads=MLA_HEADS, q_per_kv=1, dq=2 * LANES, dv=MLA_V, k_col0=0, v_col0=0)
            x = _matmul_residual(o.reshape(T, -1), w['mla_w_out'][j], x, tm=512, tn=D)
        else:
            p = _norm_matmul(x, w['norm_mix'][i], w['gqa_w_in'][j], BF16)
            t = jnp.arange(L)
            tabs = _rope_tables((t // GRID_W).astype(F32), (t % GRID_W).astype(F32), GQA_ROPE_BASE)
            nqk = GQA_HEADS + GQA_KV_HEADS
            qk = _head_norm_rope(p, L, nqk, w['gqa_gains'][j], tabs)
            o = _attention(qk.reshape(B, L, -1), qk.reshape(B, L, -1), p.reshape(B, L, -1),
                           n_heads=GQA_HEADS, q_per_kv=GQA_HEADS // GQA_KV_HEADS, dq=GQA_HD, dv=GQA_HD,
                           k_col0=GQA_HEADS * GQA_HD, v_col0=nqk * GQA_HD)
            x = _matmul_residual(o.reshape(T, -1), w['gqa_w_out'][j], x, tm=512, tn=D)
        kv = _norm_matmul(mem, w['norm_memtok'][i], w['mem_w_kv'][i], BF16)
        x = _xattn(x.reshape(B, L, D), kv.reshape(B, M, -1), w['norm_mem'][i], w['mem_w_q'][i],
                   w['mem_qk_norm'][i, 0], w['mem_qk_norm'][i, 1], w['mem_w_out'][i]).reshape(T, D)
        mlp_w = (w['norm_mlp'][i], w['mlp_w1'], w['mlp_w2'], i)
        if i < depth - 1:
            x = _mlp(x, *mlp_w)
    outs = []
    row0 = 0
    for nb in group_batches:
        outs.append(_mlp(x, *mlp_w, row0=row0, rows=nb * L).reshape(nb, L, D))
        row0 += nb * L
    return tuple(outs)


def _prepare_weights(w):
    out = dict(w)
    for name in ('ret_w_in', 'ret_w_out', 'hg_w_in', 'hg_w_out', 'mla_w_out', 'gqa_w_in', 'gqa_w_out',
                 'mem_w_q', 'mem_w_kv', 'mem_w_out', 'mlp_w1', 'mlp_w2', 'mla_w_kvb'):
        out[name] = w[name].astype(BF16)
    n = w['mla_w_in'].shape[0]
    lat = MLA_Q_RANK + MLA_KV_RANK
    out['mla_w_in'] = jnp.concatenate([w['mla_w_in'][..., :lat], _spread_pairs(w['mla_w_in'][..., lat:])],
                                      axis=-1).astype(BF16)
    wq = w['mla_w_qb'].reshape(n, MLA_Q_RANK, MLA_HEADS, MLA_NOPE + MLA_ROPE)
    wq = jnp.concatenate([wq[..., :MLA_NOPE], _spread_pairs(wq[..., MLA_NOPE:])], axis=-1)
    out['mla_w_qb'] = wq.reshape(n, MLA_Q_RANK, -1).astype(BF16)
    g = w['mla_qk_norm'].astype(F32)
    g = jnp.concatenate([g[..., :MLA_NOPE], _spread_pairs(g[..., MLA_NOPE:])], axis=-1)
    out['mla_gq'] = g[:, 0:1] * ((MLA_NOPE + MLA_ROPE) ** -0.5 * LOG2_E)
    out['mla_gk'] = g[:, 1:2]
    nqk = GQA_HEADS + GQA_KV_HEADS
    wi = w['gqa_w_in']
    wqk = _interleave_halves(wi[..., :nqk * GQA_HD].reshape(wi.shape[0], wi.shape[1], nqk, GQA_HD))
    out['gqa_w_in'] = jnp.concatenate([wqk.reshape(wi.shape[0], wi.shape[1], -1), wi[..., nqk * GQA_HD:]],
                                      axis=-1).astype(BF16)
    g = _interleave_halves(w['gqa_qk_norm'].astype(F32))
    out['gqa_gains'] = jnp.concatenate([jnp.tile(g[:, 0] * (GQA_HD ** -0.5 * LOG2_E), (1, GQA_HEADS)),
                                        jnp.tile(g[:, 1], (1, GQA_KV_HEADS))], axis=-1)[:, None, :]
    return out


def kernel(x_prompt, x_sample, mem_prompt, mem_sample, norm_mix, norm_mem, norm_memtok, norm_mlp, ret_w_in, ret_decay, ret_out_norm, ret_w_out, hg_w_in, hg_lb, hg_out_norm, hg_w_out, mla_w_in, mla_q_norm, mla_kv_norm, mla_w_qb, mla_w_kvb, mla_qk_norm, mla_w_out, gqa_w_in, gqa_qk_norm, gqa_w_out, mem_w_q, mem_w_kv, mem_qk_norm, mem_w_out, mlp_w1, mlp_w2):
    w = _prepare_weights(dict(
        norm_mix=norm_mix, norm_mem=norm_mem, norm_memtok=norm_memtok, norm_mlp=norm_mlp,
        ret_w_in=ret_w_in, ret_decay=ret_decay, ret_out_norm=ret_out_norm, ret_w_out=ret_w_out,
        hg_w_in=hg_w_in, hg_lb=hg_lb, hg_out_norm=hg_out_norm, hg_w_out=hg_w_out,
        mla_w_in=mla_w_in, mla_q_norm=mla_q_norm, mla_kv_norm=mla_kv_norm, mla_w_qb=mla_w_qb,
        mla_w_kvb=mla_w_kvb, mla_qk_norm=mla_qk_norm, mla_w_out=mla_w_out,
        gqa_w_in=gqa_w_in, gqa_qk_norm=gqa_qk_norm, gqa_w_out=gqa_w_out,
        mem_w_q=mem_w_q, mem_w_kv=mem_w_kv, mem_qk_norm=mem_qk_norm, mem_w_out=mem_w_out,
        mlp_w1=mlp_w1, mlp_w2=mlp_w2))
    mem = jnp.concatenate([mem_prompt, mem_sample], axis=0)
    return _trunk([x_prompt, x_sample], mem, w)
```

```python
import functools

import jax
import jax.numpy as jnp
from jax import lax
from jax.experimental import pallas as pl
from jax.experimental.pallas import tpu as pltpu

F32 = jnp.float32
BF16 = jnp.bfloat16

D_MODEL = 2048
GRID_W = 64
NORM_EPS = 1e-6
LOG2_E = 1.4426950408889634
RET_HEADS, RET_DK, RET_DV = 8, 256, 512
RET_ROPE_BASE = 10000.0
HG_HEADS, HG_DK, HG_DV, HG_CHUNK = 16, 128, 128, 32
MLA_HEADS, MLA_Q_RANK, MLA_KV_RANK, MLA_NOPE, MLA_ROPE, MLA_V = 16, 512, 512, 128, 64, 128
MLA_ROPE_BASE = 10000.0
GQA_HEADS, GQA_KV_HEADS, GQA_HD = 16, 4, 128
GQA_ROPE_BASE = 10000.0
MEM_HEADS, MEM_HD = 4, 128

LANES = 128
VMEM_LIMIT_BYTES = 56 * 2 ** 20

HG_BLOCK = 128
HG_HEADS_PER_STEP = 4
PREP_PARTS = 4
MLA_HEADS_PER_STEP = 8
ATTN_HEADS_PER_STEP = 8
RET_HEADS_PER_STEP = 1
MLP_FF_CHUNK = 1024


def _params(sem):
    return pltpu.CompilerParams(dimension_semantics=sem, vmem_limit_bytes=VMEM_LIMIT_BYTES)


def _tile(n, pref):
    return pref if n % pref == 0 else n


def _rms_scale(x, width=None):
    width = x.shape[-1] if width is None else width
    ms = jnp.sum(x * x, axis=-1, keepdims=True) * (1.0 / width)
    return x * lax.rsqrt(ms + NORM_EPS)


def _inv_rms_lanes(x, width):
    sq = (x * x).astype(BF16)
    ss = _dot(sq, jnp.ones((x.shape[-1], LANES), BF16))
    return lax.rsqrt(ss * (1.0 / width) + NORM_EPS)


def _sigmoid(x):
    return 1.0 / (1.0 + jnp.exp(-x))


def _dot(a, b):
    return jnp.dot(a, b, preferred_element_type=F32)


def _dot_nt(a, b):
    return lax.dot_general(a, b, (((1,), (1,)), ((), ())), preferred_element_type=F32)


def _rope_pairs(x, c, s):
    return x * c + pltpu.roll(x, LANES // 2, 1) * s


def _row_parts(xs, tm):
    bounds, lo = [], 0
    for x in xs:
        assert x.shape[0] % tm == 0
        bounds.append((lo, x.shape[0] // tm))
        lo += x.shape[0] // tm

    def index_map(part, col):
        lo, n = bounds[part]
        return lambda i, j: (jnp.clip(i - lo, 0, n - 1), jnp.where((i >= lo) & (i < lo + n), col(j), 0))

    return bounds, index_map


def _read_part(bounds, x_refs):
    i = pl.program_id(0)
    x = x_refs[-1][...]
    for part in range(len(bounds) - 2, -1, -1):
        lo, n = bounds[part]
        x = jnp.where((i >= lo) & (i < lo + n), x_refs[part][...], x)
    return x


def _norm_matmul_kernel(*refs, bounds):
    x_refs = refs[:len(bounds)]
    g_ref, w_ref, o_ref, xn_ref = refs[len(bounds):]

    def normalise(part):
        x = x_refs[part][...].astype(F32)
        xn_ref[...] = (_rms_scale(x) * g_ref[...]).astype(BF16)

    @pl.when(pl.program_id(1) == 0)
    def _():
        i = pl.program_id(0)
        for part, (lo, n) in enumerate(bounds):
            if len(bounds) == 1:
                normalise(part)
            else:
                pl.when((i >= lo) & (i < lo + n))(functools.partial(normalise, part))

    o_ref[...] = _dot(xn_ref[...], w_ref[...]).astype(o_ref.dtype)


def _norm_matmul(xs, g, w, out_dtype, tm=1024, tn=1024):
    xs = xs if isinstance(xs, (list, tuple)) else [xs]
    T, K = sum(x.shape[0] for x in xs), xs[0].shape[1]
    N = w.shape[1]
    tm, tn = _tile(min(x.shape[0] for x in xs), tm), _tile(N, tn)
    bounds, index_map = _row_parts(xs, tm)
    return pl.pallas_call(
        functools.partial(_norm_matmul_kernel, bounds=bounds),
        grid=(T // tm, N // tn),
        in_specs=[pl.BlockSpec((tm, K), index_map(p, lambda j: 0)) for p in range(len(xs))] + [
                  pl.BlockSpec((1, K), lambda i, j: (0, 0)),
                  pl.BlockSpec((K, tn), lambda i, j: (0, j))],
        out_specs=pl.BlockSpec((tm, tn), lambda i, j: (i, j)),
        out_shape=jax.ShapeDtypeStruct((T, N), out_dtype),
        scratch_shapes=[pltpu.VMEM((tm, K), BF16)],
        compiler_params=_params(("parallel", "arbitrary")),
        name="norm_matmul",
    )(*xs, g.reshape(1, K).astype(F32), w)


def _rmsnorm_kernel(*refs, bounds):
    x_refs = refs[:len(bounds)]
    g_ref, o_ref = refs[len(bounds):]

    def normalise(part):
        x = x_refs[part][...].astype(F32)
        o_ref[...] = (_rms_scale(x) * g_ref[...]).astype(o_ref.dtype)

    i = pl.program_id(0)
    for part, (lo, n) in enumerate(bounds):
        if len(bounds) == 1:
            normalise(part)
        else:
            pl.when((i >= lo) & (i < lo + n))(functools.partial(normalise, part))


def _rmsnorm(xs, g, tm=1024):
    xs = xs if isinstance(xs, (list, tuple)) else [xs]
    T, K = sum(x.shape[0] for x in xs), xs[0].shape[1]
    tm = _tile(min(x.shape[0] for x in xs), tm)
    bounds, index_map = _row_parts(xs, tm)
    return pl.pallas_call(
        functools.partial(_rmsnorm_kernel, bounds=bounds),
        grid=(T // tm, 1),
        in_specs=[pl.BlockSpec((tm, K), index_map(p, lambda j: 0)) for p in range(len(xs))] + [
                  pl.BlockSpec((1, K), lambda i, j: (0, 0))],
        out_specs=pl.BlockSpec((tm, K), lambda i, j: (i, 0)),
        out_shape=jax.ShapeDtypeStruct((T, K), BF16),
        compiler_params=_params(("parallel", "arbitrary")),
        name="rmsnorm",
    )(*xs, g.reshape(1, K).astype(F32))


def _matmul_ws_kernel(a_ref, w_ref, o_ref):
    o_ref[...] = _dot(a_ref[...], w_ref[...]).astype(o_ref.dtype)


def _matmul_ws(a, w, out_dtype, tm=1024, tn=2048):
    T, K = a.shape
    N = w.shape[1]
    tm, tn = _tile(T, tm), _tile(N, tn)
    return pl.pallas_call(
        _matmul_ws_kernel,
        grid=(N // tn, T // tm),
        in_specs=[pl.BlockSpec((tm, K), lambda j, i: (i, 0)),
                  pl.BlockSpec((K, tn), lambda j, i: (0, j))],
        out_specs=pl.BlockSpec((tm, tn), lambda j, i: (i, j)),
        out_shape=jax.ShapeDtypeStruct((T, N), out_dtype),
        compiler_params=_params(("parallel", "arbitrary")),
        name="matmul_ws",
    )(a, w)


def _matmul_residual_kernel(a_ref, w_ref, *refs, bounds):
    x_refs, o_ref = refs[:len(bounds)], refs[len(bounds)]
    o_ref[...] = _read_part(bounds, x_refs) + _dot(a_ref[...], w_ref[...])


def _matmul_residual(a, w, xs, tm=1024, tn=512):
    xs = xs if isinstance(xs, (list, tuple)) else [xs]
    T, K = a.shape
    N = w.shape[1]
    tm, tn = _tile(min(x.shape[0] for x in xs), tm), _tile(N, tn)
    bounds, index_map = _row_parts(xs, tm)
    return pl.pallas_call(
        functools.partial(_matmul_residual_kernel, bounds=bounds),
        grid=(T // tm, N // tn),
        in_specs=[pl.BlockSpec((tm, K), lambda i, j: (i, 0)),
                  pl.BlockSpec((K, tn), lambda i, j: (0, j))] + [
                  pl.BlockSpec((tm, tn), index_map(p, lambda j: j)) for p in range(len(xs))],
        out_specs=pl.BlockSpec((tm, tn), lambda i, j: (i, j)),
        out_shape=jax.ShapeDtypeStruct((T, N), F32),
        input_output_aliases={2: 0} if len(xs) == 1 else {},
        compiler_params=_params(("parallel", "arbitrary")),
        name="matmul_residual",
    )(a, w, *xs)


def _mlp_kernel(x_ref, g_ref, w1_ref, w2_ref, o_ref, xn_ref):
    @pl.when(pl.program_id(1) == 0)
    def _():
        x = x_ref[...]
        xn_ref[...] = (_rms_scale(x) * g_ref[...]).astype(BF16)
        o_ref[...] = x

    tf = w1_ref.shape[1]
    chunk = min(MLP_FF_CHUNK, tf)
    for c in range(tf // chunk):
        cs = slice(c * chunk, (c + 1) * chunk)
        a = jnp.maximum(_dot(xn_ref[...], w1_ref[:, cs]), 0.0)
        o_ref[...] += _dot((a * a).astype(BF16), w2_ref[cs, :])


def _mlp(x, g, w1, w2, layer, row0=0, rows=None, tm=512, tf=2048):
    D = x.shape[1]
    rows = x.shape[0] if rows is None else rows
    Fd = w1.shape[2]
    tm, tf = _tile(rows, tm), _tile(Fd, tf)
    assert row0 % tm == 0
    blk0 = row0 // tm
    return pl.pallas_call(
        _mlp_kernel,
        grid=(rows // tm, Fd // tf),
        in_specs=[pl.BlockSpec((tm, D), lambda i, f: (blk0 + i, 0)),
                  pl.BlockSpec((1, D), lambda i, f: (0, 0)),
                  pl.BlockSpec((None, D, tf), lambda i, f: (layer, 0, f)),
                  pl.BlockSpec((None, tf, D), lambda i, f: (layer, f, 0))],
        out_specs=pl.BlockSpec((tm, D), lambda i, f: (i, 0)),
        out_shape=jax.ShapeDtypeStruct((rows, D), F32),
        scratch_shapes=[pltpu.VMEM((tm, D), BF16)],
        compiler_params=_params(("parallel", "arbitrary")),
        name="mlp",
    )(x, g.reshape(1, D).astype(F32), w1, w2)


def _xattn_kernel(x_ref, g_ref, wq_ref, kv_ref, gq_ref, gk_ref, wo_ref, o_ref):
    x = x_ref[0]
    xn = (_rms_scale(x) * g_ref[...]).astype(BF16)
    q = _dot(xn, wq_ref[...])
    kv = kv_ref[0].astype(F32)
    width = MEM_HEADS * MEM_HD
    outs = []
    for h in range(MEM_HEADS):
        cs = slice(h * MEM_HD, (h + 1) * MEM_HD)
        qh = (_rms_scale(q[:, cs]) * gq_ref[...]).astype(BF16)
        kh = (_rms_scale(kv[:, cs]) * gk_ref[...]).astype(BF16)
        vh = kv_ref[0, :, width + h * MEM_HD: width + (h + 1) * MEM_HD]
        s = _dot_nt(qh, kh)
        p = jnp.exp(s - jnp.max(s, axis=-1, keepdims=True))
        l = jnp.sum(p, axis=-1, keepdims=True)
        outs.append((_dot(p.astype(BF16), vh) / l).astype(BF16))
    o = jnp.concatenate(outs, axis=-1)
    o_ref[0] = x + _dot(o, wo_ref[...])


def _xattn(x, kv, g, wq, gq, gk, wo, tm=1024):
    B, L, D = x.shape
    M = kv.shape[1]
    width = MEM_HEADS * MEM_HD
    tm = _tile(L, tm)
    return pl.pallas_call(
        _xattn_kernel,
        grid=(B, L // tm),
        in_specs=[pl.BlockSpec((1, tm, D), lambda b, i: (b, i, 0)),
                  pl.BlockSpec((1, D), lambda b, i: (0, 0)),
                  pl.BlockSpec((D, width), lambda b, i: (0, 0)),
                  pl.BlockSpec((1, M, 2 * width), lambda b, i: (b, 0, 0)),
                  pl.BlockSpec((1, MEM_HD), lambda b, i: (0, 0)),
                  pl.BlockSpec((1, MEM_HD), lambda b, i: (0, 0)),
                  pl.BlockSpec((width, D), lambda b, i: (0, 0))],
        out_specs=pl.BlockSpec((1, tm, D), lambda b, i: (b, i, 0)),
        out_shape=jax.ShapeDtypeStruct((B, L, D), F32),
        input_output_aliases={0: 0},
        compiler_params=_params(("parallel", "arbitrary")),
        name="mem_xattn",
    )(x, g.reshape(1, D).astype(F32), wq, kv,
      (gq.astype(F32) * (MEM_HD ** -0.5)).reshape(1, MEM_HD), gk.reshape(1, MEM_HD).astype(F32), wo)


def _attn_kernel(q_ref, k_ref, v_ref, o_ref, *, heads, q_per_kv, dq, dv):
    def scores(g):
        kg = g // q_per_kv
        return _dot_nt(q_ref[0, :, g * dq:(g + 1) * dq], k_ref[0, :, kg * dq:(kg + 1) * dq])

    s_next = scores(0)
    for g in range(heads):
        s, s_next = s_next, (scores(g + 1) if g + 1 < heads else None)
        kg = g // q_per_kv
        v = v_ref[0, :, kg * dv:(kg + 1) * dv]
        p = jnp.exp2(s - jnp.max(s, axis=-1, keepdims=True))
        l = jnp.sum(p, axis=-1, keepdims=True)
        o_ref[0, :, g * dv:(g + 1) * dv] = (_dot(p.astype(BF16), v) / l).astype(o_ref.dtype)


def _attention(q, k, v, *, n_heads, q_per_kv, dq, dv, k_col0, v_col0, tq=512):
    B, L = q.shape[0], q.shape[1]
    tq = _tile(L, tq)
    heads = ATTN_HEADS_PER_STEP
    kvh = heads // q_per_kv
    k_block0, v_block0 = k_col0 // (kvh * dq), v_col0 // (kvh * dv)
    assert n_heads % heads == 0 and k_col0 % (kvh * dq) == 0 and v_col0 % (kvh * dv) == 0
    return pl.pallas_call(
        functools.partial(_attn_kernel, heads=heads, q_per_kv=q_per_kv, dq=dq, dv=dv),
        grid=(B, n_heads // heads, L // tq),
        in_specs=[pl.BlockSpec((1, tq, heads * dq), lambda b, h, i: (b, i, h)),
                  pl.BlockSpec((1, L, kvh * dq), lambda b, h, i: (b, 0, k_block0 + h)),
                  pl.BlockSpec((1, L, kvh * dv), lambda b, h, i: (b, 0, v_block0 + h))],
        out_specs=pl.BlockSpec((1, tq, heads * dv), lambda b, h, i: (b, i, h)),
        out_shape=jax.ShapeDtypeStruct((B, L, n_heads * dv), BF16),
        compiler_params=_params(("parallel", "parallel", "arbitrary")),
        name="softmax_attention",
    )(q, k, v)


def _retention_kernel(q_ref, k_ref, v_ref, g_ref, cos_ref, sin_ref, dec_ref, on_ref, o_ref, *, tq):
    for h in range(RET_HEADS_PER_STEP):
        qk = pl.ds(h * RET_DK, RET_DK)
        vg = pl.ds(h * RET_DV, RET_DV)
        _retention_head(q_ref.at[:, :, qk], k_ref.at[:, :, qk], v_ref.at[:, :, vg], g_ref.at[:, :, vg],
                        cos_ref, sin_ref, dec_ref.at[pl.ds(h, 1)], on_ref, o_ref.at[:, :, vg], tq=tq)


def _retention_head(q_ref, k_ref, v_ref, g_ref, cos_ref, sin_ref, dec_ref, on_ref, o_ref, *, tq):
    L = q_ref.shape[1]
    half = RET_DK // 2
    cos = cos_ref[...]
    sin = sin_ref[...]

    def rope(ref, scale):
        x1 = ref[0, :, :half].astype(F32)
        x2 = ref[0, :, half:].astype(F32)
        y = jnp.concatenate([x1 * cos - x2 * sin, x1 * sin + x2 * cos], axis=-1)
        return (y * scale).astype(BF16)

    q = rope(q_ref, RET_DK ** -0.5)
    k = rope(k_ref, 1.0)
    v = v_ref[0]
    dec = dec_ref[0]
    lg = jnp.minimum(dec, 0.0) - jnp.log1p(jnp.exp(-jnp.abs(dec)))
    lg_f = lg[0:1, 0:1]
    lg_b = lg[1:2, 0:1]
    n = L // tq
    ri = lax.broadcasted_iota(jnp.int32, (tq, tq), 0)
    diff = (ri - lax.broadcasted_iota(jnp.int32, (tq, tq), 1)).astype(F32)
    d_intra = jnp.where(diff == 0, 2.0, jnp.exp(jnp.where(diff >= 0, diff * lg_f, -diff * lg_b)))
    r = lax.broadcasted_iota(jnp.int32, (tq, RET_DK), 0).astype(F32)
    qdec = (jnp.exp((r + 1.0) * lg_f), jnp.exp((tq - r) * lg_b))
    kdec = (jnp.exp((tq - 1.0 - r) * lg_f), jnp.exp(r * lg_b))
    gc = (jnp.exp(tq * lg_f), jnp.exp(tq * lg_b))
    rows = [slice(i * tq, (i + 1) * tq) for i in range(n)]

    def kv_state(m, d):
        kd = (k[rows[m]].astype(F32) * kdec[d]).T.astype(BF16)
        return _dot(kd, v[rows[m]])

    state = [[None] * n, [None] * n]
    for i in range(1, n):
        prev = state[0][i - 1]
        state[0][i] = kv_state(i - 1, 0) if prev is None else prev * gc[0] + kv_state(i - 1, 0)
    for i in range(n - 2, -1, -1):
        nxt = state[1][i + 1]
        state[1][i] = kv_state(i + 1, 1) if nxt is None else nxt * gc[1] + kv_state(i + 1, 1)
    for i in range(n):
        qi = q[rows[i]]
        s = _dot_nt(qi, k[rows[i]]) * d_intra
        o = _dot(s.astype(BF16), v[rows[i]])
        for d in (0, 1):
            if state[d][i] is not None:
                o = o + _dot((qi.astype(F32) * qdec[d]).astype(BF16), state[d][i].astype(BF16))
        y = _rms_scale(o) * on_ref[...]
        gate = g_ref[0, rows[i], :].astype(F32)
        o_ref[0, rows[i], :] = (y * (gate * _sigmoid(gate))).astype(o_ref.dtype)


def _retention(p, cos, sin, dec, out_norm, tq=512):
    B, L, _ = p.shape
    H = RET_HEADS
    hp = RET_HEADS_PER_STEP
    tq = _tile(L, tq)
    n = H // hp
    return pl.pallas_call(
        functools.partial(_retention_kernel, tq=tq),
        grid=(n, B),
        in_specs=[pl.BlockSpec((1, L, hp * RET_DK), lambda h, b: (b, 0, h)),
                  pl.BlockSpec((1, L, hp * RET_DK), lambda h, b: (b, 0, n + h)),
                  pl.BlockSpec((1, L, hp * RET_DV), lambda h, b: (b, 0, n + h)),
                  pl.BlockSpec((1, L, hp * RET_DV), lambda h, b: (b, 0, 2 * n + h)),
                  pl.BlockSpec((L, RET_DK // 2), lambda h, b: (0, 0)),
                  pl.BlockSpec((L, RET_DK // 2), lambda h, b: (0, 0)),
                  pl.BlockSpec((hp, 2, LANES), lambda h, b: (h, 0, 0)),
                  pl.BlockSpec((1, RET_DV), lambda h, b: (0, 0))],
        out_specs=pl.BlockSpec((1, L, hp * RET_DV), lambda h, b: (b, 0, h)),
        out_shape=jax.ShapeDtypeStruct((B, L, H * RET_DV), BF16),
        compiler_params=_params(("parallel", "arbitrary")),
        name="retention",
    )(p, p, p, p, cos, sin, dec, out_norm.reshape(1, RET_DV).astype(F32))


def _hgrn_kernel(q_ref, ff_ref, fb_ref, i_ref, g_ref, lb_ref, on_ref, m_ref, o_ref,
                 of_ref, ob_ref, st_ref, qt_ref, kt_ref, ks_ref, dec_ref, *, heads):
    L = q_ref.shape[1]
    nblk = L // HG_BLOCK
    n_chunks = HG_BLOCK // HG_CHUNK
    width = heads * HG_DK
    rows = lax.broadcasted_iota(jnp.int32, (HG_BLOCK, HG_BLOCK), 0)
    cols = lax.broadcasted_iota(jnp.int32, (HG_BLOCK, HG_BLOCK), 1)
    same_chunk = (rows // HG_CHUNK) == (cols // HG_CHUNK)
    masks = (same_chunk & (cols <= rows), same_chunk & (cols >= rows))
    row_chunk = lax.broadcasted_iota(jnp.int32, (HG_BLOCK, HG_DK), 0) // HG_CHUNK
    chunk_rows = [jnp.where(row_chunk == c, 1.0, 0.0).astype(BF16) for c in range(n_chunks)]
    st_ref[...] = jnp.zeros_like(st_ref)

    def block_rows(t, d):
        r = t if d == 0 else nblk - 1 - t
        return pl.ds(pl.multiple_of(r * HG_BLOCK, HG_BLOCK), HG_BLOCK)

    def prep(t, d, slot, part):
        rs = block_rows(t, d)
        pw = width // PREP_PARTS
        cs = slice(part * pw, (part + 1) * pw)
        fz = (ff_ref if d == 0 else fb_ref)[0, rs, cs].astype(F32)
        lb = lb_ref[:, cs]
        f = lb + (1.0 - lb) * _sigmoid(fz)
        lf = jnp.log(f)
        kk = 1.0 - f
        hi = lf.astype(BF16)
        lo = (lf - hi.astype(F32)).astype(BF16)
        cb = _dot(m_ref[d], jnp.concatenate([hi, lo], axis=1))
        b = cb[:, :pw] + cb[:, pw:]
        last = [c * HG_CHUNK + (HG_CHUNK - 1 if d == 0 else 0) for c in range(n_chunks)]
        dec_rows = [jnp.exp(b[i:i + 1, :]) for i in last]
        dec = jnp.concatenate([jnp.broadcast_to(row, (HG_CHUNK, pw)) for row in dec_rows], axis=0)
        kt = kk * jnp.exp(-b)
        qt_ref[slot, d, :, cs] = (q_ref[0, rs, cs].astype(F32) * jnp.exp(b)).astype(BF16)
        kt_ref[slot, d, :, cs] = kt.astype(BF16)
        ks_ref[slot, d, :, cs] = (kt * dec).astype(BF16)
        for c in range(n_chunks):
            dec_ref[slot, d, c:c + 1, cs] = dec_rows[c]

    items = [(h, d) for h in range(heads) for d in (0, 1)]
    per_step = 2 * PREP_PARTS // n_chunks
    stride = len(items) // per_step
    assert per_step * n_chunks == 2 * PREP_PARTS and stride * per_step == len(items)

    def half(t, slot):
        rs = [block_rows(t, d) for d in (0, 1)]
        v = {(h, d): i_ref[0, rs[d], h * HG_DK:(h + 1) * HG_DK] for h, d in items}
        qt = {(h, d): qt_ref[slot, d, :, h * HG_DK:(h + 1) * HG_DK] for h, d in items}
        s = {(h, d): _dot_nt(qt[h, d], kt_ref[slot, d, :, h * HG_DK:(h + 1) * HG_DK]) for h, d in items}
        kv = {}
        for h, d in items:
            ks = ks_ref[slot, d, :, h * HG_DK:(h + 1) * HG_DK]
            ks_cols = jnp.concatenate([ks * chunk_rows[c] for c in range(n_chunks)], axis=1)
            kv[h, d] = _dot(v[h, d].astype(F32).T.astype(BF16), ks_cols)
        t_next = jnp.minimum(t + 1, nblk - 1)
        s = {it: jnp.where(masks[it[1]], s[it], 0.0).astype(BF16) for it in items}
        o = {it: _dot(s[it], v[it]) for it in items}
        st = {(h, d): st_ref[2 * h + d] for h, d in items}
        parts = {it: [None] * n_chunks for it in items}
        for step in range(n_chunks):
            for idx, (h, d) in enumerate(items):
                if idx % stride == 0:
                    piece = step * per_step + idx // stride
                    prep(t_next, piece % 2, 1 - slot, piece // 2)
                c = step if d == 0 else n_chunks - 1 - step
                cr = slice(c * HG_CHUNK, (c + 1) * HG_CHUNK)
                parts[h, d][c] = o[h, d][cr] + _dot_nt(qt[h, d][cr], st[h, d].astype(BF16))
                dec = dec_ref[slot, d, c:c + 1, h * HG_DK:(h + 1) * HG_DK]
                st[h, d] = st[h, d] * dec + kv[h, d][:, c * HG_DK:(c + 1) * HG_DK]
        for h, d in items:
            st_ref[2 * h + d] = st[h, d]
        for d in (0, 1):
            out = jnp.concatenate([jnp.concatenate(parts[h, d], axis=0) for h in range(heads)], axis=1)
            (of_ref if d == 0 else ob_ref)[rs[d], :] = out

    def body(u, carry):
        half(2 * u, 0)
        half(2 * u + 1, 1)
        return carry

    for d in (0, 1):
        for part in range(PREP_PARTS):
            prep(jnp.int32(0), d, 0, part)
    lax.fori_loop(0, nblk // 2, body, 0)
    for h in range(heads):
        cs = slice(h * HG_DV, (h + 1) * HG_DV)
        y = _rms_scale(of_ref[:, cs] + ob_ref[:, cs]) * on_ref[...]
        gate = g_ref[0, :, cs].astype(F32)
        o_ref[0, :, cs] = (y * (gate * _sigmoid(gate))).astype(o_ref.dtype)


def _hgrn(p, lb, out_norm):
    B, L, _ = p.shape
    heads = HG_HEADS_PER_STEP
    nb = HG_HEADS // heads
    w = heads * HG_DK
    c = HG_CHUNK
    n_chunks = HG_BLOCK // HG_CHUNK
    assert (L // HG_BLOCK) % 2 == 0
    idx = jnp.arange(HG_BLOCK)
    same = (idx[:, None] // c) == (idx[None, :] // c)
    tri_f = same & (idx[None, :] <= idx[:, None])
    tri_b = same & (idx[None, :] >= idx[:, None])
    m = jnp.stack([tri_f, tri_b]).astype(BF16)
    spec = lambda off: pl.BlockSpec((1, L, w), lambda b, j: (b, 0, off * nb + j))
    return pl.pallas_call(
        functools.partial(_hgrn_kernel, heads=heads),
        grid=(B, nb),
        in_specs=[spec(0), spec(1), spec(2), spec(3), spec(4),
                  pl.BlockSpec((1, w), lambda b, j: (0, j)),
                  pl.BlockSpec((1, HG_DV), lambda b, j: (0, 0)),
                  pl.BlockSpec((2, HG_BLOCK, HG_BLOCK), lambda b, j: (0, 0, 0))],
        out_specs=pl.BlockSpec((1, L, w), lambda b, j: (b, 0, j)),
        out_shape=jax.ShapeDtypeStruct((B, L, HG_HEADS * HG_DV), BF16),
        scratch_shapes=[pltpu.VMEM((L, w), F32), pltpu.VMEM((L, w), F32),
                        pltpu.VMEM((2 * heads, HG_DV, HG_DK), F32),
                        pltpu.VMEM((2, 2, HG_BLOCK, w), BF16), pltpu.VMEM((2, 2, HG_BLOCK, w), BF16),
                        pltpu.VMEM((2, 2, HG_BLOCK, w), BF16), pltpu.VMEM((2, 2, n_chunks, w), F32)],
        compiler_params=_params(("parallel", "arbitrary")),
        name="hgrn2",
    )(p, p, p, p, p, lb.reshape(1, -1).astype(F32), out_norm.reshape(1, HG_DV).astype(F32), m)


def _mla_q_kernel(c_ref, gn_ref, w_ref, gh_ref, cos_ref, sin_ref, o_ref, xn_ref):
    @pl.when(pl.program_id(1) == 0)
    def _():
        xn_ref[...] = (_rms_scale(c_ref[...]) * gn_ref[...]).astype(BF16)

    hw = 2 * LANES
    head_dot = lambda g: _dot(xn_ref[...], w_ref[:, g * hw:(g + 1) * hw])
    y_next = head_dot(0)
    for g in range(MLA_HEADS_PER_STEP):
        y, y_next = y_next, (head_dot(g + 1) if g + 1 < MLA_HEADS_PER_STEP else None)
        inv = _inv_rms_lanes(y, MLA_NOPE + MLA_ROPE)
        gh = gh_ref[...]
        o_ref[:, g * hw:g * hw + MLA_NOPE] = (y[:, :MLA_NOPE] * inv * gh[:, :MLA_NOPE]).astype(o_ref.dtype)
        o_ref[:, g * hw + MLA_NOPE:(g + 1) * hw] = _rope_pairs(
            y[:, MLA_NOPE:] * inv * gh[:, MLA_NOPE:], cos_ref[...], sin_ref[...]).astype(o_ref.dtype)


def _mla_kv_kernel(c_ref, kr_ref, gn_ref, w_ref, gh_ref, cos_ref, sin_ref, k_ref, v_ref, xn_ref):
    @pl.when(pl.program_id(1) == 0)
    def _():
        xn_ref[...] = (_rms_scale(c_ref[...]) * gn_ref[...]).astype(BF16)

    hw = 2 * LANES
    kr = kr_ref[...]
    gh = gh_ref[...]
    head_dot = lambda g: _dot(xn_ref[...], w_ref[:, g * hw:(g + 1) * hw])
    y_next = head_dot(0)
    for g in range(MLA_HEADS_PER_STEP):
        y, y_next = y_next, (head_dot(g + 1) if g + 1 < MLA_HEADS_PER_STEP else None)
        kn = y[:, :MLA_NOPE]
        inv = _inv_rms_lanes(jnp.concatenate([kn, kr], axis=1), MLA_NOPE + MLA_ROPE)
        k_ref[:, g * hw:g * hw + MLA_NOPE] = (kn * inv * gh[:, :MLA_NOPE]).astype(k_ref.dtype)
        k_ref[:, g * hw + MLA_NOPE:(g + 1) * hw] = _rope_pairs(
            kr * inv * gh[:, MLA_NOPE:], cos_ref[...], sin_ref[...]).astype(k_ref.dtype)
        v_ref[:, g * MLA_V:(g + 1) * MLA_V] = y[:, MLA_NOPE:].astype(v_ref.dtype)


def _mla_qkv(c, L, q_norm, kv_norm, wq, wkv, gq, gk, tabs, tm=1024):
    T = c.shape[0]
    H = MLA_HEADS
    hps = MLA_HEADS_PER_STEP
    tm = _tile(L, tm)
    lt = L // tm
    hw = 2 * LANES
    tab_spec = pl.BlockSpec((tm, LANES), lambda i, h: (i % lt, 0))
    row = lambda n: pl.BlockSpec((1, n), lambda i, h: (0, 0))
    q = pl.pallas_call(
        _mla_q_kernel,
        grid=(T // tm, H // hps),
        in_specs=[pl.BlockSpec((tm, MLA_Q_RANK), lambda i, h: (i, 0)), row(MLA_Q_RANK),
                  pl.BlockSpec((MLA_Q_RANK, hps * hw), lambda i, h: (0, h)), row(hw),
                  tab_spec, tab_spec],
        out_specs=pl.BlockSpec((tm, hps * hw), lambda i, h: (i, h)),
        out_shape=jax.ShapeDtypeStruct((T, H * hw), BF16),
        scratch_shapes=[pltpu.VMEM((tm, MLA_Q_RANK), BF16)],
        compiler_params=_params(("parallel", "arbitrary")),
        name="mla_q",
    )(c, q_norm.reshape(1, -1).astype(F32), wq, gq, *tabs)
    k, v = pl.pallas_call(
        _mla_kv_kernel,
        grid=(T // tm, H // hps),
        in_specs=[pl.BlockSpec((tm, MLA_KV_RANK), lambda i, h: (i, 1)),
                  pl.BlockSpec((tm, LANES), lambda i, h: (i, (MLA_Q_RANK + MLA_KV_RANK) // LANES)),
                  row(MLA_KV_RANK),
                  pl.BlockSpec((MLA_KV_RANK, hps * hw), lambda i, h: (0, h)), row(hw),
                  tab_spec, tab_spec],
        out_specs=[pl.BlockSpec((tm, hps * hw), lambda i, h: (i, h)),
                   pl.BlockSpec((tm, hps * MLA_V), lambda i, h: (i, h))],
        out_shape=[jax.ShapeDtypeStruct((T, H * hw), BF16), jax.ShapeDtypeStruct((T, H * MLA_V), BF16)],
        scratch_shapes=[pltpu.VMEM((tm, MLA_KV_RANK), BF16)],
        compiler_params=_params(("parallel", "arbitrary")),
        name="mla_kv",
    )(c, c, kv_norm.reshape(1, -1).astype(F32), wkv, gk, *tabs)
    return q, k, v


def _head_norm_rope_kernel(x_ref, g_ref, cos_ref, sin_ref, o_ref, *, n_heads):
    for h in range(n_heads):
        cs = slice(h * LANES, (h + 1) * LANES)
        x = x_ref[:, cs].astype(F32)
        y = x * _inv_rms_lanes(x, LANES) * g_ref[:, cs]
        o_ref[:, cs] = _rope_pairs(y, cos_ref[...], sin_ref[...]).astype(o_ref.dtype)


def _head_norm_rope(p, L, n_heads, gains, tabs, tm=512):
    T = p.shape[0]
    tm = _tile(L, tm)
    lt = L // tm
    w = n_heads * LANES
    tab_spec = pl.BlockSpec((tm, LANES), lambda i: (i % lt, 0))
    return pl.pallas_call(
        functools.partial(_head_norm_rope_kernel, n_heads=n_heads),
        grid=(T // tm,),
        in_specs=[pl.BlockSpec((tm, w), lambda i: (i, 0)),
                  pl.BlockSpec((1, w), lambda i: (0, 0)),
                  tab_spec, tab_spec],
        out_specs=pl.BlockSpec((tm, w), lambda i: (i, 0)),
        out_shape=jax.ShapeDtypeStruct((T, w), BF16),
        compiler_params=_params(("parallel",)),
        name="head_norm_rope",
    )(p, gains, *tabs)


ROPE_HALF = 32


def _rope_tables(pos_a, pos_b, base):
    freqs = base ** (-jnp.arange(ROPE_HALF, dtype=F32) / ROPE_HALF)

    def cs(pos):
        ang = pos[:, None] * freqs[None, :]
        return jnp.cos(ang), jnp.sin(ang)

    ca, sa = cs(pos_a)
    cb, sb = (jnp.zeros_like(ca), jnp.zeros_like(sa)) if pos_b is None else cs(pos_b)
    return jnp.concatenate([ca, cb, ca, cb], axis=-1), jnp.concatenate([-sa, -sb, sa, sb], axis=-1)


def _spread_pairs(a):
    z = jnp.zeros(a.shape[:-1] + (ROPE_HALF,), a.dtype)
    return jnp.concatenate([a[..., :ROPE_HALF], z, a[..., ROPE_HALF:], z], axis=-1)


def _interleave_halves(a):
    q = [a[..., i * ROPE_HALF:(i + 1) * ROPE_HALF] for i in range(4)]
    return jnp.concatenate([q[0], q[2], q[1], q[3]], axis=-1)


def _trunk(xs, mem, w):
    group_batches = [x.shape[0] for x in xs]
    B, (L, D) = sum(group_batches), xs[0].shape[1:]
    T = B * L
    M = mem.shape[1]
    x = [x.reshape(-1, D) for x in xs]
    mem = mem.reshape(B * M, D)
    pos = jnp.arange(L, dtype=F32)
    sm = jax.nn.softmax(w['hg_lb'].astype(F32), axis=0)
    lb_all = jnp.cumsum(sm, axis=0) - sm[0]
    depth = w['norm_mix'].shape[0]
    for i in range(depth):
        kind, j = i % 4, i // 4
        if kind == 0:
            p = _matmul_ws(_rmsnorm(x, w['norm_mix'][i]), w['ret_w_in'][j], BF16)
            half = RET_DK // 2
            ang = pos[:, None] * (RET_ROPE_BASE ** (-jnp.arange(half, dtype=F32) / half))[None, :]
            dec = jnp.broadcast_to(w['ret_decay'][j].astype(F32).T[:, :, None], (RET_HEADS, 2, LANES))
            o = _retention(p.reshape(B, L, -1), jnp.cos(ang), jnp.sin(ang), dec, w['ret_out_norm'][j])
            x = _matmul_residual(o.reshape(T, -1), w['ret_w_out'][j], x)
        elif kind == 1:
            p = _norm_matmul(x, w['norm_mix'][i], w['hg_w_in'][j], BF16, tn=2048)
            o = _hgrn(p.reshape(B, L, -1), lb_all[i], w['hg_out_norm'][j])
            x = _matmul_residual(o.reshape(T, -1), w['hg_w_out'][j], x, tm=512, tn=D)
        elif kind == 2:
            c = _norm_matmul(x, w['norm_mix'][i], w['mla_w_in'][j], F32)
            tabs = _rope_tables(pos, None, MLA_ROPE_BASE)
            q, k, v = _mla_qkv(c, L, w['mla_q_norm'][j], w['mla_kv_norm'][j], w['mla_w_qb'][j], w['mla_w_kvb'][j],
                               w['mla_gq'][j], w['mla_gk'][j], tabs)
            o = _attention(q.reshape(B, L, -1), k.reshape(B, L, -1), v.reshape(B, L, -1),
                           n_heads=MLA_HEADS, q_per_kv=1, dq=2 * LANES, dv=MLA_V, k_col0=0, v_col0=0)
            x = _matmul_residual(o.reshape(T, -1), w['mla_w_out'][j], x, tm=512, tn=D)
        else:
            p = _norm_matmul(x, w['norm_mix'][i], w['gqa_w_in'][j], BF16)
            t = jnp.arange(L)
            tabs = _rope_tables((t // GRID_W).astype(F32), (t % GRID_W).astype(F32), GQA_ROPE_BASE)
            nqk = GQA_HEADS + GQA_KV_HEADS
            qk = _head_norm_rope(p, L, nqk, w['gqa_gains'][j], tabs)
            o = _attention(qk.reshape(B, L, -1), qk.reshape(B, L, -1), p.reshape(B, L, -1),
                           n_heads=GQA_HEADS, q_per_kv=GQA_HEADS // GQA_KV_HEADS, dq=GQA_HD, dv=GQA_HD,
                           k_col0=GQA_HEADS * GQA_HD, v_col0=nqk * GQA_HD)
            x = _matmul_residual(o.reshape(T, -1), w['gqa_w_out'][j], x, tm=512, tn=D)
        kv = _norm_matmul(mem, w['norm_memtok'][i], w['mem_w_kv'][i], BF16)
        x = _xattn(x.reshape(B, L, D), kv.reshape(B, M, -1), w['norm_mem'][i], w['mem_w_q'][i],
                   w['mem_qk_norm'][i, 0], w['mem_qk_norm'][i, 1], w['mem_w_out'][i]).reshape(T, D)
        mlp_w = (w['norm_mlp'][i], w['mlp_w1'], w['mlp_w2'], i)
        if i < depth - 1:
            x = _mlp(x, *mlp_w)
    outs = []
    row0 = 0
    for nb in group_batches:
        outs.append(_mlp(x, *mlp_w, row0=row0, rows=nb * L).reshape(nb, L, D))
        row0 += nb * L
    return tuple(outs)


def _prepare_weights(w):
    out = dict(w)
    for name in ('ret_w_in', 'ret_w_out', 'hg_w_in', 'hg_w_out', 'mla_w_out', 'gqa_w_in', 'gqa_w_out',
                 'mem_w_q', 'mem_w_kv', 'mem_w_out', 'mlp_w1', 'mlp_w2', 'mla_w_kvb'):
        out[name] = w[name].astype(BF16)
    n = w['mla_w_in'].shape[0]
    lat = MLA_Q_RANK + MLA_KV_RANK
    out['mla_w_in'] = jnp.concatenate([w['mla_w_in'][..., :lat], _spread_pairs(w['mla_w_in'][..., lat:])],
                                      axis=-1).astype(BF16)
    wq = w['mla_w_qb'].reshape(n, MLA_Q_RANK, MLA_HEADS, MLA_NOPE + MLA_ROPE)
    wq = jnp.concatenate([wq[..., :MLA_NOPE], _spread_pairs(wq[..., MLA_NOPE:])], axis=-1)
    out['mla_w_qb'] = wq.reshape(n, MLA_Q_RANK, -1).astype(BF16)
    g = w['mla_qk_norm'].astype(F32)
    g = jnp.concatenate([g[..., :MLA_NOPE], _spread_pairs(g[..., MLA_NOPE:])], axis=-1)
    out['mla_gq'] = g[:, 0:1] * ((MLA_NOPE + MLA_ROPE) ** -0.5 * LOG2_E)
    out['mla_gk'] = g[:, 1:2]
    nqk = GQA_HEADS + GQA_KV_HEADS
    wi = w['gqa_w_in']
    wqk = _interleave_halves(wi[..., :nqk * GQA_HD].reshape(wi.shape[0], wi.shape[1], nqk, GQA_HD))
    out['gqa_w_in'] = jnp.concatenate([wqk.reshape(wi.shape[0], wi.shape[1], -1), wi[..., nqk * GQA_HD:]],
                                      axis=-1).astype(BF16)
    g = _interleave_halves(w['gqa_qk_norm'].astype(F32))
    out['gqa_gains'] = jnp.concatenate([jnp.tile(g[:, 0] * (GQA_HD ** -0.5 * LOG2_E), (1, GQA_HEADS)),
                                        jnp.tile(g[:, 1], (1, GQA_KV_HEADS))], axis=-1)[:, None, :]
    return out


def kernel(x_prompt, x_sample, mem_prompt, mem_sample, norm_mix, norm_mem, norm_memtok, norm_mlp, ret_w_in, ret_decay, ret_out_norm, ret_w_out, hg_w_in, hg_lb, hg_out_norm, hg_w_out, mla_w_in, mla_q_norm, mla_kv_norm, mla_w_qb, mla_w_kvb, mla_qk_norm, mla_w_out, gqa_w_in, gqa_qk_norm, gqa_w_out, mem_w_q, mem_w_kv, mem_qk_norm, mem_w_out, mlp_w1, mlp_w2):
    w = _prepare_weights(dict(
        norm_mix=norm_mix, norm_mem=norm_mem, norm_memtok=norm_memtok, norm_mlp=norm_mlp,
        ret_w_in=ret_w_in, ret_decay=ret_decay, ret_out_norm=ret_out_norm, ret_w_out=ret_w_out,
        hg_w_in=hg_w_in, hg_lb=hg_lb, hg_out_norm=hg_out_norm, hg_w_out=hg_w_out,
        mla_w_in=mla_w_in, mla_q_norm=mla_q_norm, mla_kv_norm=mla_kv_norm, mla_w_qb=mla_w_qb,
        mla_w_kvb=mla_w_kvb, mla_qk_norm=mla_qk_norm, mla_w_out=mla_w_out,
        gqa_w_in=gqa_w_in, gqa_qk_norm=gqa_qk_norm, gqa_w_out=gqa_w_out,
        mem_w_q=mem_w_q, mem_w_kv=mem_w_kv, mem_qk_norm=mem_qk_norm, mem_w_out=mem_w_out,
        mlp_w1=mlp_w1, mlp_w2=mlp_w2))
    mem = jnp.concatenate([mem_prompt, mem_sample], axis=0)
    return _trunk([x_prompt, x_sample], mem, w)
```
